```python
import math
import jax
import jax.numpy as jnp
from jax import lax
import numpy as np


D_MODEL = 2048
BATCH = 4
SEQ = 2048
DEPTH = 4

N_MIXERS = 2
N_MLA = (DEPTH + N_MIXERS - 1) // N_MIXERS
N_HGRN = DEPTH // N_MIXERS
RMS_EPS = 1e-6

MLA_HEADS = 16
Q_LORA = 512
KV_LORA = 512
QK_NOPE = 128
QK_ROPE = 64
QK_HEAD = QK_NOPE + QK_ROPE
V_HEAD = 128
MLA_A_DIM = Q_LORA + KV_LORA + QK_ROPE
ROPE_THETA = 10000.0
Q_BLOCK = 128

HGRN_HEADS = 16
HGRN_EXPAND = 128
HGRN_HEAD_V = D_MODEL // HGRN_HEADS
HGRN_FDIM = HGRN_HEADS * HGRN_EXPAND
HGRN_VDIM = HGRN_HEADS * HGRN_HEAD_V
HGRN_IN_DIM = 2 * HGRN_FDIM + 2 * HGRN_VDIM
CHUNK = 64

D_FF = 5632
CONV_W = 3

PLE_DIM = 256

kernel_name = 'hybrid_mla_hgrn2_convffn_ple'


def rms_norm(x, g):
    xf = x.astype(jnp.float32)
    y = xf * lax.rsqrt(jnp.mean(xf * xf, axis=-1, keepdims=True) + RMS_EPS)
    return (y * g.astype(jnp.float32)).astype(x.dtype)


def rope_tables(positions):
    inv_freq = ROPE_THETA ** (-jnp.arange(0, QK_ROPE, 2, dtype=jnp.float32) / QK_ROPE)
    ang = positions.astype(jnp.float32)[..., None] * inv_freq
    return jnp.cos(ang), jnp.sin(ang)


def apply_rope(x, cos, sin):
    c = cos[:, :, None, :].astype(x.dtype)
    s = sin[:, :, None, :].astype(x.dtype)
    x1, x2 = jnp.split(x, 2, axis=-1)
    return jnp.concatenate([x1 * c - x2 * s, x2 * c + x1 * s], axis=-1)


def causal_block_attention(q, k, v):
    B, S, H, dq = q.shape
    dv = v.shape[-1]
    nb = S // Q_BLOCK
    scale = 1.0 / math.sqrt(dq)
    kt = k.transpose(0, 2, 1, 3)
    vt = v.transpose(0, 2, 1, 3)
    qb = q.reshape(B, nb, Q_BLOCK, H, dq).transpose(1, 0, 3, 2, 4)
    key_pos = jnp.arange(S)

    def one_block(args):
        qblk, bi = args
        s = jnp.einsum('bhqd,bhkd->bhqk', qblk, kt).astype(jnp.float32) * scale
        q_pos = bi * Q_BLOCK + jnp.arange(Q_BLOCK)
        mask = key_pos[None, :] <= q_pos[:, None]
        s = jnp.where(mask, s, -jnp.inf)
        pr = jax.nn.softmax(s, axis=-1).astype(vt.dtype)
        return jnp.einsum('bhqk,bhkd->bhqd', pr, vt)

    o = lax.map(one_block, (qb, jnp.arange(nb)))
    return o.transpose(1, 0, 3, 2, 4).reshape(B, S, H * dv)


def mla_mixer(xn, cos, sin, w_a, q_norm, w_uq, kv_norm, w_ukv, qk_gain_q, qk_gain_k, w_o):
    B, S, _ = xn.shape
    c = xn @ w_a
    c_q = c[..., :Q_LORA]
    c_kv = c[..., Q_LORA:Q_LORA + KV_LORA]
    k_pe = c[..., Q_LORA + KV_LORA:]
    q = (rms_norm(c_q, q_norm) @ w_uq).reshape(B, S, MLA_HEADS, QK_HEAD)
    kv = (rms_norm(c_kv, kv_norm) @ w_ukv).reshape(B, S, MLA_HEADS, QK_NOPE + V_HEAD)
    k_nope = kv[..., :QK_NOPE]
    v = kv[..., QK_NOPE:]
    k = jnp.concatenate([k_nope, jnp.broadcast_to(k_pe[:, :, None, :], (B, S, MLA_HEADS, QK_ROPE))], axis=-1)
    q = rms_norm(q, qk_gain_q)
    k = rms_norm(k, qk_gain_k)
    q = jnp.concatenate([q[..., :QK_NOPE], apply_rope(q[..., QK_NOPE:], cos, sin)], axis=-1)
    k = jnp.concatenate([k[..., :QK_NOPE], apply_rope(k[..., QK_NOPE:], cos, sin)], axis=-1)
    o = causal_block_attention(q, k, v)
    return o @ w_o


def _hgrn_chunk_step(state, inp):
    q, k, v, log_f = inp
    b = jnp.cumsum(log_f, axis=2)
    o_inter = jnp.einsum('bhtd,bhde->bhte', q * jnp.exp(b), state)
    diff = b[:, :, :, None, :] - b[:, :, None, :, :]
    causal = jnp.tril(jnp.ones((CHUNK, CHUNK), dtype=bool))
    decay = jnp.exp(jnp.where(causal[:, :, None], diff, -jnp.inf))
    a = jnp.einsum('bhtd,bhsd,bhtsd->bhts', q, k, decay)
    o = o_inter + jnp.einsum('bhts,bhse->bhte', a, v)
    b_last = b[:, :, -1:, :]
    new_state = jnp.exp(b_last[:, :, 0, :])[..., None] * state + jnp.einsum('bhsd,bhse->bhde', k * jnp.exp(b_last - b), v)
    return new_state, o


def hgrn2_mixer(xn, lb, w_in, o_norm, w_o):
    B, T, _ = xn.shape
    nc = T // CHUNK
    f32 = jnp.float32
    z = xn @ w_in
    q_raw = z[..., :HGRN_FDIM]
    f_raw = z[..., HGRN_FDIM:2 * HGRN_FDIM]
    i_in = z[..., 2 * HGRN_FDIM:2 * HGRN_FDIM + HGRN_VDIM]
    g = z[..., 2 * HGRN_FDIM + HGRN_VDIM:]
    lbf = lb.astype(f32)
    ff = f_raw.astype(f32)
    log_f = jnp.log(lbf + (1.0 - lbf) * jax.nn.sigmoid(ff))
    k = (1.0 - lbf) * jax.nn.sigmoid(-ff)
    q = jax.nn.silu(q_raw.astype(f32))
    v = i_in.astype(f32)

    def to_chunks(t, d):
        return t.reshape(B, nc, CHUNK, HGRN_HEADS, d).transpose(1, 0, 3, 2, 4)

    s0 = jnp.zeros((B, HGRN_HEADS, HGRN_EXPAND, HGRN_HEAD_V), f32)
    _, o = lax.scan(_hgrn_chunk_step, s0, (to_chunks(q, HGRN_EXPAND), to_chunks(k, HGRN_EXPAND), to_chunks(v, HGRN_HEAD_V), to_chunks(log_f, HGRN_EXPAND)))
    o = o.transpose(1, 0, 3, 2, 4).reshape(B, T, HGRN_HEADS, HGRN_HEAD_V)
    o = rms_norm(o, o_norm.reshape(HGRN_HEADS, HGRN_HEAD_V)).astype(xn.dtype).reshape(B, T, HGRN_VDIM)
    o = o * jax.nn.silu(g)
    return o @ w_o


def conv_ffn(xn, w_in, conv_w, conv_b, w_down):
    T = xn.shape[1]
    gu = xn @ w_in
    gate = gu[..., :D_FF]
    up = gu[..., D_FF:]
    gp = jnp.pad(gate, ((0, 0), (CONV_W - 1, 0), (0, 0)))
    conv = conv_b + gp[:, 0:T] * conv_w[0]
    for tap in range(1, CONV_W):
        conv = conv + gp[:, tap:tap + T] * conv_w[tap]
    return (jax.nn.silu(conv) * up) @ w_down


def per_layer_embedding(h, p_i, norm_g, w_proj, w_gate):
    return (p_i @ w_proj) * jax.nn.sigmoid(rms_norm(h, norm_g) @ w_gate)


def _w(k, shape, fan_in):
    return jax.random.normal(k, shape, jnp.float32) * (fan_in ** -0.5)


def _gain(k, shape):
    return 1.0 + 0.02 * jax.random.normal(k, shape, jnp.float32)


def setup_inputs(seed: int = 0) -> dict:
    key = jax.random.key(seed)
    ks = jax.random.split(key, 26)
    x = jax.random.normal(ks[0], (BATCH, SEQ, D_MODEL), jnp.float32)
    p = jax.random.normal(ks[1], (DEPTH, BATCH, SEQ, PLE_DIM), jnp.float32)
    positions = (jnp.arange(SEQ, dtype=jnp.int32)[None, :] + jax.random.randint(ks[2], (BATCH, 1), 0, 4096, dtype=jnp.int32)).astype(jnp.int32)
    return {
        'x': x,
        'p': p,
        'positions': positions,
        'mix_norm': _gain(ks[3], (DEPTH, D_MODEL)),
        'ffn_norm': _gain(ks[4], (DEPTH, D_MODEL)),
        'ple_norm': _gain(ks[5], (DEPTH, D_MODEL)),
        'mla_w_a': _w(ks[6], (N_MLA, D_MODEL, MLA_A_DIM), D_MODEL),
        'mla_q_norm': _gain(ks[7], (N_MLA, Q_LORA)),
        'mla_w_uq': _w(ks[8], (N_MLA, Q_LORA, MLA_HEADS * QK_HEAD), Q_LORA),
        'mla_kv_norm': _gain(ks[9], (N_MLA, KV_LORA)),
        'mla_w_ukv': _w(ks[10], (N_MLA, KV_LORA, MLA_HEADS * (QK_NOPE + V_HEAD)), KV_LORA),
        'mla_qk_gain_q': _gain(ks[11], (N_MLA, QK_HEAD)),
        'mla_qk_gain_k': _gain(ks[12], (N_MLA, QK_HEAD)),
        'mla_w_o': _w(ks[13], (N_MLA, MLA_HEADS * V_HEAD, D_MODEL), MLA_HEADS * V_HEAD),
        'hgrn_lb_logits': 0.5 * jax.random.normal(ks[14], (DEPTH, HGRN_FDIM), jnp.float32),
        'hgrn_w_in': _w(ks[15], (N_HGRN, D_MODEL, HGRN_IN_DIM), D_MODEL),
        'hgrn_o_norm': _gain(ks[16], (N_HGRN, HGRN_VDIM)),
        'hgrn_w_o': _w(ks[17], (N_HGRN, HGRN_VDIM, D_MODEL), HGRN_VDIM),
        'ffn_w_in': _w(ks[18], (DEPTH, D_MODEL, 2 * D_FF), D_MODEL),
        'ffn_conv_w': _w(ks[19], (DEPTH, CONV_W, D_FF), CONV_W),
        'ffn_conv_b': 0.01 * jax.random.normal(ks[20], (DEPTH, D_FF), jnp.float32),
        'ffn_w_down': _w(ks[21], (DEPTH, D_FF, D_MODEL), D_FF),
        'ple_w_proj': _w(ks[22], (DEPTH, PLE_DIM, D_MODEL), PLE_DIM),
        'ple_w_gate': _w(ks[23], (DEPTH, D_MODEL, D_MODEL), D_MODEL),
    }


def reference(x, p, positions, mix_norm, ffn_norm, ple_norm, mla_w_a, mla_q_norm, mla_w_uq, mla_kv_norm, mla_w_ukv, mla_qk_gain_q, mla_qk_gain_k, mla_w_o, hgrn_lb_logits, hgrn_w_in, hgrn_o_norm, hgrn_w_o, ffn_w_in, ffn_conv_w, ffn_conv_b, ffn_w_down, ple_w_proj, ple_w_gate):
    cos, sin = rope_tables(positions)
    sm = jax.nn.softmax(hgrn_lb_logits.astype(jnp.float32), axis=0)
    lower_bounds = jnp.cumsum(sm, axis=0) - sm[0]
    h = x
    for i in range(DEPTH):
        j = i // N_MIXERS
        xn = rms_norm(h, mix_norm[i])
        if i % N_MIXERS == 0:
            h = h + mla_mixer(xn, cos, sin, mla_w_a[j], mla_q_norm[j], mla_w_uq[j], mla_kv_norm[j], mla_w_ukv[j], mla_qk_gain_q[j], mla_qk_gain_k[j], mla_w_o[j])
        else:
            h = h + hgrn2_mixer(xn, lower_bounds[i], hgrn_w_in[j], hgrn_o_norm[j], hgrn_w_o[j])
        h = h + conv_ffn(rms_norm(h, ffn_norm[i]), ffn_w_in[i], ffn_conv_w[i], ffn_conv_b[i], ffn_w_down[i])
        h = h + per_layer_embedding(h, p[i], ple_norm[i], ple_w_proj[i], ple_w_gate[i])
    return h
```

```python
import functools
import math

import jax
import jax.numpy as jnp
from jax import lax
from jax.experimental import pallas as pl
from jax.experimental.pallas import tpu as pltpu

F32 = jnp.float32
BF16 = jnp.bfloat16

RMS_EPS = 1e-6
ROPE_THETA = 10000.0

LANES = 128
BF16_SUBLANES = 16
VMEM_LIMIT_BYTES = 56 * 1024 * 1024

MLA_HEADS = 16
Q_LORA = 512
KV_LORA = 512
QK_NOPE = 128
QK_ROPE = 64
QK_HEAD = QK_NOPE + QK_ROPE
V_HEAD = 128
HEAD_PAD = 256

HGRN_HEADS = 16
HGRN_DK = 128
HGRN_DV = 128
CHUNK = 64
SUB = 16
HALF = SUB // 2

CONV_W = 3
CONV_HALO = BF16_SUBLANES


def _params(*sem):
    return pltpu.CompilerParams(dimension_semantics=sem, vmem_limit_bytes=VMEM_LIMIT_BYTES)


def _rms(x, g):
    return x * lax.rsqrt(jnp.mean(x * x, axis=-1, keepdims=True) + RMS_EPS) * g


def _sigmoid(x):
    return 1.0 / (1.0 + jnp.exp(-x))


def _dot(a, b):
    return jnp.dot(a, b, preferred_element_type=F32)


def _dot_nt(a, b):
    return lax.dot_general(a, b, (((1,), (1,)), ((), ())), preferred_element_type=F32)


def _dot_tn(a, b):
    return lax.dot_general(a, b, (((0,), (0,)), ((), ())), preferred_element_type=F32)


def _matmul_kernel(*refs, norm, residual):
    it = iter(refs)
    x_ref = next(it)
    g_ref = next(it) if norm else None
    w_ref = next(it)
    r_ref = next(it) if residual else None
    o_ref = next(it)
    xn_ref = next(it) if norm else None

    if norm:
        @pl.when(pl.program_id(1) == 0)
        def _():
            xn_ref[...] = _rms(x_ref[...], g_ref[...]).astype(BF16)
        acc = _dot(xn_ref[...], w_ref[...])
    else:
        acc = _dot(x_ref[...], w_ref[...])
    if residual:
        acc = r_ref[...] + acc
    o_ref[...] = acc.astype(o_ref.dtype)


def _matmul(x, w, *, gain=None, residual=None, out_dtype, tm, tn, name):
    m, k = x.shape
    n = w.shape[1]
    tm, tn = min(tm, m), min(tn, n)
    assert m % tm == 0 and n % tn == 0
    norm = gain is not None
    ins = [x]
    specs = [pl.BlockSpec((tm, k), lambda i, j: (i, 0))]
    if norm:
        ins.append(gain.reshape(1, k))
        specs.append(pl.BlockSpec((1, k), lambda i, j: (0, 0)))
    ins.append(w)
    specs.append(pl.BlockSpec((k, tn), lambda i, j: (0, j)))
    if residual is not None:
        ins.append(residual)
        specs.append(pl.BlockSpec((tm, tn), lambda i, j: (i, j)))
    return pl.pallas_call(
        functools.partial(_matmul_kernel, norm=norm, residual=residual is not None),
        grid=(m // tm, n // tn),
        in_specs=specs,
        out_specs=pl.BlockSpec((tm, tn), lambda i, j: (i, j)),
        out_shape=jax.ShapeDtypeStruct((m, n), out_dtype),
        scratch_shapes=[pltpu.VMEM((tm, k), BF16)] if norm else [],
        compiler_params=_params("parallel", "arbitrary"),
        name=name,
    )(*ins)


def _ffn_in_kernel(x_ref, halo_ref, g_ref, wg_ref, wu_ref, cw_ref, cb_ref, o_ref, xn_ref, gate_ref,
                   *, tm, tiles_per_seq):
    i = pl.program_id(0)

    @pl.when(pl.program_id(1) == 0)
    def _():
        g = g_ref[...]
        keep = jnp.where(i % tiles_per_seq != 0, 1.0, 0.0)
        xn_ref[:CONV_HALO, :] = (_rms(halo_ref[...], g) * keep).astype(BF16)
        xn_ref[CONV_HALO:, :] = _rms(x_ref[...], g).astype(BF16)

    gate_ref[...] = _dot(xn_ref[...], wg_ref[...])
    up = _dot(xn_ref[CONV_HALO:, :], wu_ref[...])
    cw = cw_ref[...]
    conv = cb_ref[...] + gate_ref[pl.ds(CONV_HALO - 2, tm), :] * cw[0:1, :]
    conv = conv + gate_ref[pl.ds(CONV_HALO - 1, tm), :] * cw[1:2, :]
    conv = conv + gate_ref[pl.ds(CONV_HALO, tm), :] * cw[2:3, :]
    o_ref[...] = (conv * _sigmoid(conv) * up).astype(o_ref.dtype)


def _ffn_in(h, gain, w_in, conv_w, conv_b, *, seq, tm, tn):
    m, k = h.shape
    d_ff = w_in.shape[1] // 2
    tm = min(tm, seq)
    assert seq % tm == 0 and d_ff % tn == 0 and tm % CONV_HALO == 0
    nj = d_ff // tn
    halo_blocks = tm // CONV_HALO
    return pl.pallas_call(
        functools.partial(_ffn_in_kernel, tm=tm, tiles_per_seq=seq // tm),
        grid=(m // tm, nj),
        in_specs=[
            pl.BlockSpec((tm, k), lambda i, j: (i, 0)),
            pl.BlockSpec((CONV_HALO, k), lambda i, j: (jnp.maximum(i * halo_blocks - 1, 0), 0)),
            pl.BlockSpec((1, k), lambda i, j: (0, 0)),
            pl.BlockSpec((k, tn), lambda i, j: (0, j)),
            pl.BlockSpec((k, tn), lambda i, j: (0, j + nj)),
            pl.BlockSpec((CONV_W, tn), lambda i, j: (0, j)),
            pl.BlockSpec((1, tn), lambda i, j: (0, j)),
        ],
        out_specs=pl.BlockSpec((tm, tn), lambda i, j: (i, j)),
        out_shape=jax.ShapeDtypeStruct((m, d_ff), BF16),
        scratch_shapes=[pltpu.VMEM((tm + CONV_HALO, k), BF16), pltpu.VMEM((tm + CONV_HALO, tn), F32)],
        compiler_params=_params("parallel", "arbitrary"),
        name="ffn_in",
    )(h, h, gain.reshape(1, k), w_in, w_in, conv_w, conv_b.reshape(1, d_ff))


def _ple_kernel(h_ref, g_ref, wg_ref, p_ref, wp_ref, o_ref, xn_ref, *, tn):
    j = pl.program_id(1)

    @pl.when(j == 0)
    def _():
        xn_ref[...] = _rms(h_ref[...], g_ref[...]).astype(BF16)

    gate = _dot(xn_ref[...], wg_ref[...])
    proj = _dot(p_ref[...].astype(BF16), wp_ref[...])
    res = h_ref[:, pl.ds(pl.multiple_of(j * tn, tn), tn)]
    o_ref[...] = res + proj * _sigmoid(gate)


def _ple(h, gain, w_gate, p, w_proj, *, tm, tn):
    m, k = h.shape
    n = w_gate.shape[1]
    kp = p.shape[1]
    tm, tn = min(tm, m), min(tn, n)
    assert m % tm == 0 and n % tn == 0 and n == k
    return pl.pallas_call(
        functools.partial(_ple_kernel, tn=tn),
        grid=(m // tm, n // tn),
        in_specs=[
            pl.BlockSpec((tm, k), lambda i, j: (i, 0)),
            pl.BlockSpec((1, k), lambda i, j: (0, 0)),
            pl.BlockSpec((k, tn), lambda i, j: (0, j)),
            pl.BlockSpec((tm, kp), lambda i, j: (i, 0)),
            pl.BlockSpec((kp, tn), lambda i, j: (0, j)),
        ],
        out_specs=pl.BlockSpec((tm, tn), lambda i, j: (i, j)),
        out_shape=jax.ShapeDtypeStruct((m, n), F32),
        scratch_shapes=[pltpu.VMEM((tm, k), BF16)],
        compiler_params=_params("parallel", "arbitrary"),
        name="ple",
    )(h, gain.reshape(1, k), w_gate, p, w_proj)


def _rope_table_kernel(pos_ref, inv_ref, c_ref, sa_ref, sb_ref):
    ang = pos_ref[...].astype(F32) * inv_ref[...]
    lane = lax.broadcasted_iota(jnp.int32, ang.shape, 1)
    c = jnp.cos(ang)
    s = jnp.sin(ang)
    half = QK_ROPE // 2
    c_ref[...] = jnp.where(lane < QK_ROPE, c, 0.0)
    sa_ref[...] = jnp.where(lane < half, -s, 0.0)
    sb_ref[...] = jnp.where((lane >= half) & (lane < QK_ROPE), s, 0.0)


def _rope_tables(positions, *, tm):
    m = positions.size
    tm = min(tm, m)
    half = QK_ROPE // 2
    inv_freq = ROPE_THETA ** (-jnp.arange(0, QK_ROPE, 2, dtype=F32) / QK_ROPE)
    inv_row = jnp.concatenate([inv_freq, inv_freq, jnp.zeros((LANES - 2 * half,), F32)]).reshape(1, LANES)
    out = jax.ShapeDtypeStruct((m, LANES), F32)
    spec = pl.BlockSpec((tm, LANES), lambda i: (i, 0))
    return pl.pallas_call(
        _rope_table_kernel,
        grid=(m // tm,),
        in_specs=[pl.BlockSpec((tm, 1), lambda i: (i, 0)), pl.BlockSpec((1, LANES), lambda i: (0, 0))],
        out_specs=[spec, spec, spec],
        out_shape=[out, out, out],
        compiler_params=_params("parallel"),
        name="rope_tables",
    )(positions.reshape(m, 1), inv_row)


def _mla_proj_kernel(h_ref, g_ref, wa_ref, qn_ref, kvn_ref, wuq_ref, wukv_ref, gq_ref, gk_ref,
                     c_ref, sa_ref, sb_ref, q_out, k_out, v_out):
    xn = _rms(h_ref[...], g_ref[...]).astype(BF16)
    c = _dot(xn, wa_ref[...])
    cq = _rms(c[:, :Q_LORA], qn_ref[...]).astype(BF16)
    ckv = _rms(c[:, Q_LORA:Q_LORA + KV_LORA], kvn_ref[...]).astype(BF16)
    k_pe = c[:, Q_LORA + KV_LORA:]
    cos_t, sin_a, sin_b = c_ref[...], sa_ref[...], sb_ref[...]
    half = QK_ROPE // 2

    def rope(r):
        return r * cos_t + pltpu.roll(r, LANES - half, 1) * sin_a + pltpu.roll(r, half, 1) * sin_b

    gq = gq_ref[...]
    gk = gk_ref[...]
    k_pe_sq = jnp.sum(k_pe * k_pe, axis=-1, keepdims=True)
    k_rope = rope(k_pe * gk[:, QK_NOPE:])

    for hd in range(MLA_HEADS):
        lo = hd * HEAD_PAD
        mid = lo + QK_NOPE
        hi = lo + HEAD_PAD
        qh = _dot(cq, wuq_ref[:, lo:hi])
        inv = lax.rsqrt(jnp.sum(qh * qh, axis=-1, keepdims=True) * (1.0 / QK_HEAD) + RMS_EPS)
        qh = qh * inv * gq
        q_out[:, lo:mid] = qh[:, :QK_NOPE].astype(BF16)
        q_out[:, mid:hi] = rope(qh[:, QK_NOPE:]).astype(BF16)

        kvh = _dot(ckv, wukv_ref[:, lo:hi])
        kn = kvh[:, :QK_NOPE]
        inv_k = lax.rsqrt((jnp.sum(kn * kn, axis=-1, keepdims=True) + k_pe_sq) * (1.0 / QK_HEAD) + RMS_EPS)
        k_out[:, lo:mid] = (kn * inv_k * gk[:, :QK_NOPE]).astype(BF16)
        k_out[:, mid:hi] = (k_rope * inv_k).astype(BF16)
        v_out[:, hd * V_HEAD:(hd + 1) * V_HEAD] = kvh[:, QK_NOPE:].astype(BF16)


def _mla_proj(h, gain, w_a, q_norm, w_uq_pad, kv_norm, w_ukv, gq_pad, gk_pad, tables, *, tm):
    m, k = h.shape
    tm = min(tm, m)
    assert m % tm == 0
    a_dim = w_a.shape[1]
    nq = MLA_HEADS * HEAD_PAD
    const = lambda i: (0, 0)
    row = lambda i: (i, 0)
    return pl.pallas_call(
        _mla_proj_kernel,
        grid=(m // tm,),
        in_specs=[
            pl.BlockSpec((tm, k), row),
            pl.BlockSpec((1, k), const),
            pl.BlockSpec((k, a_dim), const),
            pl.BlockSpec((1, Q_LORA), const),
            pl.BlockSpec((1, KV_LORA), const),
            pl.BlockSpec((Q_LORA, nq), const),
            pl.BlockSpec((KV_LORA, nq), const),
            pl.BlockSpec((1, HEAD_PAD), const),
            pl.BlockSpec((1, HEAD_PAD), const),
            pl.BlockSpec((tm, LANES), row),
            pl.BlockSpec((tm, LANES), row),
            pl.BlockSpec((tm, LANES), row),
        ],
        out_specs=[pl.BlockSpec((tm, nq), row), pl.BlockSpec((tm, nq), row),
                   pl.BlockSpec((tm, MLA_HEADS * V_HEAD), row)],
        out_shape=[jax.ShapeDtypeStruct((m, nq), BF16), jax.ShapeDtypeStruct((m, nq), BF16),
                   jax.ShapeDtypeStruct((m, MLA_HEADS * V_HEAD), BF16)],
        compiler_params=_params("parallel"),
        name="mla_proj",
    )(h, gain.reshape(1, k), w_a, q_norm.reshape(1, Q_LORA), kv_norm.reshape(1, KV_LORA),
      w_uq_pad, w_ukv, gq_pad, gk_pad, *tables)


def _attn_kernel(q_ref, k_ref, v_ref, o_ref, *, blk):
    seq = q_ref.shape[0]
    row = lax.broadcasted_iota(jnp.int32, (blk, blk), 0)
    col = lax.broadcasted_iota(jnp.int32, (blk, blk), 1)
    causal = col <= row
    for qi in range(seq // blk):
        q = q_ref[qi * blk:(qi + 1) * blk, :]
        m_run = jnp.full((blk, 1), -jnp.inf, F32)
        l_run = jnp.zeros((blk, 1), F32)
        acc = jnp.zeros((blk, V_HEAD), F32)
        for ki in range(qi + 1):
            s = _dot_nt(q, k_ref[ki * blk:(ki + 1) * blk, :])
            if ki == qi:
                s = jnp.where(causal, s, -jnp.inf)
            m_new = jnp.maximum(m_run, jnp.max(s, axis=-1, keepdims=True))
            p = jnp.exp(s - m_new)
            alpha = jnp.exp(m_run - m_new)
            l_run = alpha * l_run + jnp.sum(p, axis=-1, keepdims=True)
            acc = alpha * acc + _dot(p.astype(BF16), v_ref[ki * blk:(ki + 1) * blk, :])
            m_run = m_new
        o_ref[qi * blk:(qi + 1) * blk, :] = (acc / l_run).astype(o_ref.dtype)


def _attention(q, k, v, *, batch, seq, blk):
    m = q.shape[0]
    blk = min(blk, seq)
    assert seq % blk == 0
    return pl.pallas_call(
        functools.partial(_attn_kernel, blk=blk),
        grid=(batch, MLA_HEADS),
        in_specs=[
            pl.BlockSpec((seq, HEAD_PAD), lambda b, h: (b, h)),
            pl.BlockSpec((seq, HEAD_PAD), lambda b, h: (b, h)),
            pl.BlockSpec((seq, V_HEAD), lambda b, h: (b, h)),
        ],
        out_specs=pl.BlockSpec((seq, V_HEAD), lambda b, h: (b, h)),
        out_shape=jax.ShapeDtypeStruct((m, MLA_HEADS * V_HEAD), BF16),
        compiler_params=_params("parallel", "parallel"),
        name="mla_attention",
    )(q, k, v)


def _hgrn_chunk(q_raw, ff, v, gate, lb, o_gain, st, tri):
    f = lb + (1.0 - lb) * _sigmoid(ff)
    log_f = jnp.log(f)
    k = (1.0 - lb) * _sigmoid(-ff)
    q = q_raw * _sigmoid(q_raw)
    v16 = v.astype(BF16)

    hi = log_f.astype(BF16)
    r1 = log_f - hi.astype(F32)
    mid = r1.astype(BF16)
    lo = (r1 - mid.astype(F32)).astype(BF16)
    b = _dot(tri, hi) + _dot(tri, mid) + _dot(tri, lo)

    o = _dot_nt((q * jnp.exp(b)).astype(BF16), st.astype(BF16))

    lane = lax.broadcasted_iota(jnp.int32, (HALF, CHUNK), 1)
    sub_t = lax.broadcasted_iota(jnp.int32, (HALF, HGRN_DK), 0)
    srow = lax.broadcasted_iota(jnp.int32, (CHUNK, HGRN_DK), 0)
    a_rows = []
    for blk in range(CHUNK // SUB):
        r0 = blk * SUB
        halves = []
        for hf in range(2):
            t0 = r0 + hf * HALF
            q_t = q[t0:t0 + HALF, :]
            b_t = b[t0:t0 + HALF, :]
            a_half = jnp.zeros((HALF, CHUNK), F32)
            for sl in range((hf + 1) * HALF):
                s = r0 + sl
                diff = b_t - b[s:s + 1, :]
                if sl >= hf * HALF:
                    diff = jnp.where(sub_t >= sl - hf * HALF, diff, -jnp.inf)
                col = jnp.sum(q_t * k[s:s + 1, :] * jnp.exp(diff), axis=-1, keepdims=True)
                a_half = jnp.where(lane == s, col, a_half)
            halves.append(a_half)
        a_blk = jnp.concatenate(halves, axis=0)
        if blk > 0:
            ref_b = b[r0 - 1:r0, :]
            q_f = (q[r0:r0 + SUB, :] * jnp.exp(b[r0:r0 + SUB, :] - ref_b)).astype(BF16)
            k_f = jnp.where(srow < r0, k * jnp.exp(jnp.minimum(ref_b - b, 0.0)), 0.0).astype(BF16)
            a_blk = a_blk + _dot_nt(q_f, k_f)
        a_rows.append(a_blk)
    a = jnp.concatenate(a_rows, axis=0)
    o = o + _dot(a.astype(BF16), v16)

    b_last = b[CHUNK - 1:CHUNK, :]
    k_d = (k * jnp.exp(b_last - b)).astype(BF16)
    st_new = st * jnp.exp(b_last) + _dot_tn(v16, k_d)

    o = o * lax.rsqrt(jnp.mean(o * o, axis=-1, keepdims=True) + RMS_EPS) * o_gain
    return o * (gate * _sigmoid(gate)), st_new


def _hgrn_kernel(q_ref, f_ref, v_ref, g_ref, lb_ref, on_ref, o_ref, st_ref, *, heads, chunks):
    @pl.when(pl.program_id(2) == 0)
    def _():
        st_ref[...] = jnp.zeros_like(st_ref)

    r = lax.broadcasted_iota(jnp.int32, (CHUNK, CHUNK), 0)
    c = lax.broadcasted_iota(jnp.int32, (CHUNK, CHUNK), 1)
    tri = (c <= r).astype(BF16)

    def body(ci, carry):
        rows = pl.ds(pl.multiple_of(ci * CHUNK, CHUNK), CHUNK)
        for hd in range(heads):
            cols = slice(hd * HGRN_DK, (hd + 1) * HGRN_DK)
            o, st_new = _hgrn_chunk(q_ref[rows, cols], f_ref[rows, cols], v_ref[rows, cols], g_ref[rows, cols],
                                    lb_ref[:, cols], on_ref[:, cols], st_ref[hd], tri)
            st_ref[hd] = st_new
            o_ref[rows, cols] = o.astype(o_ref.dtype)
        return carry

    lax.fori_loop(0, chunks, body, 0)


def _hgrn_recurrence(z, lb, o_gain, *, batch, seq, heads, tb):
    m = z.shape[0]
    fdim = HGRN_HEADS * HGRN_DK
    tb = min(tb, seq)
    assert seq % tb == 0 and tb % CHUNK == 0 and HGRN_HEADS % heads == 0
    width = heads * HGRN_DK
    groups = HGRN_HEADS // heads
    nt = seq // tb

    def zspec(part):
        return pl.BlockSpec((tb, width), lambda b, g, t: (b * nt + t, part * groups + g))

    return pl.pallas_call(
        functools.partial(_hgrn_kernel, heads=heads, chunks=tb // CHUNK),
        grid=(batch, groups, nt),
        in_specs=[zspec(0), zspec(1), zspec(2), zspec(3),
                  pl.BlockSpec((1, width), lambda b, g, t: (0, g)),
                  pl.BlockSpec((1, width), lambda b, g, t: (0, g))],
        out_specs=pl.BlockSpec((tb, width), lambda b, g, t: (b * nt + t, g)),
        out_shape=jax.ShapeDtypeStruct((m, fdim), BF16),
        scratch_shapes=[pltpu.VMEM((heads, HGRN_DV, HGRN_DK), F32)],
        compiler_params=_params("parallel", "parallel", "arbitrary"),
        name="hgrn_recurrence",
    )(z, z, z, z, lb.reshape(1, fdim), o_gain.reshape(1, fdim))


def _mla_layer(h, tables, mix_gain, w_a, q_norm, w_uq, kv_norm, w_ukv, gain_q, gain_k, w_o, *, batch, seq):
    scale = 1.0 / math.sqrt(QK_HEAD)
    pad = HEAD_PAD - QK_HEAD
    w_uq_pad = jnp.pad(w_uq.reshape(Q_LORA, MLA_HEADS, QK_HEAD), ((0, 0), (0, 0), (0, pad)))
    w_uq_pad = w_uq_pad.reshape(Q_LORA, MLA_HEADS * HEAD_PAD).astype(BF16)
    gq_pad = jnp.pad(gain_q * scale, (0, pad)).reshape(1, HEAD_PAD)
    gk_pad = jnp.pad(gain_k, (0, pad)).reshape(1, HEAD_PAD)
    w_a_pad = jnp.pad(w_a, ((0, 0), (0, LANES - QK_ROPE))).astype(BF16)
    q, k, v = _mla_proj(h, mix_gain, w_a_pad, q_norm, w_uq_pad, kv_norm, w_ukv.astype(BF16),
                        gq_pad, gk_pad, tables, tm=256)
    o = _attention(q, k, v, batch=batch, seq=seq, blk=512)
    return _matmul(o, w_o.astype(BF16), residual=h, out_dtype=F32, tm=1024, tn=1024, name="mla_out")


def _hgrn_layer(h, mix_gain, lb, w_in, o_gain, w_o, *, batch, seq):
    z = _matmul(h, w_in.astype(BF16), gain=mix_gain, out_dtype=F32, tm=1024, tn=1024, name="hgrn_in")
    o = _hgrn_recurrence(z, lb, o_gain, batch=batch, seq=seq, heads=2, tb=512)
    return _matmul(o, w_o.astype(BF16), residual=h, out_dtype=F32, tm=1024, tn=1024, name="hgrn_out")


def kernel(x, p, positions, mix_norm, ffn_norm, ple_norm, mla_w_a, mla_q_norm, mla_w_uq, mla_kv_norm, mla_w_ukv, mla_qk_gain_q, mla_qk_gain_k, mla_w_o, hgrn_lb_logits, hgrn_w_in, hgrn_o_norm, hgrn_w_o, ffn_w_in, ffn_conv_w, ffn_conv_b, ffn_w_down, ple_w_proj, ple_w_gate):
    batch, seq, d_model = x.shape
    depth = p.shape[0]
    m = batch * seq
    tables = _rope_tables(positions, tm=1024)
    sm = jax.nn.softmax(hgrn_lb_logits.astype(F32), axis=0)
    lower_bounds = jnp.cumsum(sm, axis=0) - sm[0]
    h = x.reshape(m, d_model)
    for i in range(depth):
        j = i // 2
        if i % 2 == 0:
            h = _mla_layer(h, tables, mix_norm[i], mla_w_a[j], mla_q_norm[j], mla_w_uq[j], mla_kv_norm[j],
                           mla_w_ukv[j], mla_qk_gain_q[j], mla_qk_gain_k[j], mla_w_o[j], batch=batch, seq=seq)
        else:
            h = _hgrn_layer(h, mix_norm[i], lower_bounds[i], hgrn_w_in[j], hgrn_o_norm[j], hgrn_w_o[j],
                            batch=batch, seq=seq)
        act = _ffn_in(h, ffn_norm[i], ffn_w_in[i].astype(BF16), ffn_conv_w[i], ffn_conv_b[i],
                      seq=seq, tm=1024, tn=512)
        h = _matmul(act, ffn_w_down[i].astype(BF16), residual=h, out_dtype=F32, tm=1024, tn=512, name="ffn_down")
        h = _ple(h, ple_norm[i], ple_w_gate[i].astype(BF16), p[i].reshape(m, -1), ple_w_proj[i].astype(BF16),
                 tm=1024, tn=1024)
    return h.reshape(batch, seq, d_model)
```

```python
import functools
import math

import jax
import jax.numpy as jnp
from jax import lax
from jax.experimental import pallas as pl
from jax.experimental.pallas import tpu as pltpu

F32 = jnp.float32
BF16 = jnp.bfloat16

RMS_EPS = 1e-6
ROPE_THETA = 10000.0

LANES = 128
BF16_SUBLANES = 16
VMEM_LIMIT_BYTES = 56 * 1024 * 1024

MLA_HEADS = 16
Q_LORA = 512
KV_LORA = 512
QK_NOPE = 128
QK_ROPE = 64
QK_HEAD = QK_NOPE + QK_ROPE
V_HEAD = 128
HEAD_PAD = 256

HGRN_HEADS = 16
HGRN_DK = 128
HGRN_DV = 128
CHUNK = 64
SUB = 16
HALF = SUB // 2

CONV_W = 3
CONV_HALO = BF16_SUBLANES


def _params(*sem):
    return pltpu.CompilerParams(dimension_semantics=sem, vmem_limit_bytes=VMEM_LIMIT_BYTES)


def _rms(x, g):
    return x * lax.rsqrt(jnp.mean(x * x, axis=-1, keepdims=True) + RMS_EPS) * g


def _sigmoid(x):
    return 1.0 / (1.0 + jnp.exp(-x))


def _dot(a, b):
    return jnp.dot(a, b, preferred_element_type=F32)


def _dot_nt(a, b):
    return lax.dot_general(a, b, (((1,), (1,)), ((), ())), preferred_element_type=F32)


def _dot_tn(a, b):
    return lax.dot_general(a, b, (((0,), (0,)), ((), ())), preferred_element_type=F32)


def _wspec(k, tn, layer, col):
    return pl.BlockSpec((None, k, tn), lambda *g: (layer, 0, col(*g)))


def _matmul_kernel(*refs, norm, residual):
    it = iter(refs)
    x_ref = next(it)
    g_ref = next(it) if norm else None
    w_ref = next(it)
    r_ref = next(it) if residual else None
    o_ref = next(it)
    xn_ref = next(it) if norm else None

    if norm:
        @pl.when(pl.program_id(1) == 0)
        def _():
            xn_ref[...] = _rms(x_ref[...], g_ref[...]).astype(BF16)
        acc = _dot(xn_ref[...], w_ref[...])
    else:
        acc = _dot(x_ref[...], w_ref[...])
    if residual:
        acc = r_ref[...] + acc
    o_ref[...] = acc.astype(o_ref.dtype)


def _matmul(x, w, layer, *, gain=None, residual=None, out_dtype, tm, tn, name):
    m, k = x.shape
    n = w.shape[2]
    tm, tn = min(tm, m), min(tn, n)
    assert m % tm == 0 and n % tn == 0
    norm = gain is not None
    ins = [x]
    specs = [pl.BlockSpec((tm, k), lambda i, j: (i, 0))]
    if norm:
        ins.append(gain.reshape(1, k))
        specs.append(pl.BlockSpec((1, k), lambda i, j: (0, 0)))
    ins.append(w)
    specs.append(_wspec(k, tn, layer, lambda i, j: j))
    if residual is not None:
        ins.append(residual)
        specs.append(pl.BlockSpec((tm, tn), lambda i, j: (i, j)))
    return pl.pallas_call(
        functools.partial(_matmul_kernel, norm=norm, residual=residual is not None),
        grid=(m // tm, n // tn),
        in_specs=specs,
        out_specs=pl.BlockSpec((tm, tn), lambda i, j: (i, j)),
        out_shape=jax.ShapeDtypeStruct((m, n), out_dtype),
        scratch_shapes=[pltpu.VMEM((tm, k), BF16)] if norm else [],
        compiler_params=_params("parallel", "arbitrary"),
        name=name,
    )(*ins)


def _ffn_in_kernel(x_ref, halo_ref, g_ref, wg_ref, wu_ref, cw_ref, cb_ref, o_ref, xn_ref, gate_ref,
                   *, tm, tiles_per_seq):
    i = pl.program_id(0)

    @pl.when(pl.program_id(1) == 0)
    def _():
        g = g_ref[...]
        keep = jnp.where(i % tiles_per_seq != 0, 1.0, 0.0)
        xn_ref[:CONV_HALO, :] = (_rms(halo_ref[...], g) * keep).astype(BF16)
        xn_ref[CONV_HALO:, :] = _rms(x_ref[...], g).astype(BF16)

    gate_ref[...] = _dot(xn_ref[...], wg_ref[...])
    up = _dot(xn_ref[CONV_HALO:, :], wu_ref[...])
    cw = cw_ref[...]
    conv = cb_ref[...] + gate_ref[pl.ds(CONV_HALO - 2, tm), :] * cw[0:1, :]
    conv = conv + gate_ref[pl.ds(CONV_HALO - 1, tm), :] * cw[1:2, :]
    conv = conv + gate_ref[pl.ds(CONV_HALO, tm), :] * cw[2:3, :]
    o_ref[...] = (conv * _sigmoid(conv) * up).astype(o_ref.dtype)


def _ffn_in(h, gain, w_in, layer, conv_w, conv_b, *, seq, tm, tn):
    m, k = h.shape
    d_ff = w_in.shape[2] // 2
    tm = min(tm, seq)
    assert seq % tm == 0 and d_ff % tn == 0 and tm % CONV_HALO == 0
    nj = d_ff // tn
    halo_blocks = tm // CONV_HALO
    return pl.pallas_call(
        functools.partial(_ffn_in_kernel, tm=tm, tiles_per_seq=seq // tm),
        grid=(m // tm, nj),
        in_specs=[
            pl.BlockSpec((tm, k), lambda i, j: (i, 0)),
            pl.BlockSpec((CONV_HALO, k), lambda i, j: (jnp.maximum(i * halo_blocks - 1, 0), 0)),
            pl.BlockSpec((1, k), lambda i, j: (0, 0)),
            _wspec(k, tn, layer, lambda i, j: j),
            _wspec(k, tn, layer, lambda i, j: j + nj),
            pl.BlockSpec((CONV_W, tn), lambda i, j: (0, j)),
            pl.BlockSpec((1, tn), lambda i, j: (0, j)),
        ],
        out_specs=pl.BlockSpec((tm, tn), lambda i, j: (i, j)),
        out_shape=jax.ShapeDtypeStruct((m, d_ff), BF16),
        scratch_shapes=[pltpu.VMEM((tm + CONV_HALO, k), BF16), pltpu.VMEM((tm + CONV_HALO, tn), F32)],
        compiler_params=_params("parallel", "arbitrary"),
        name="ffn_in",
    )(h, h, gain.reshape(1, k), w_in, w_in, conv_w, conv_b.reshape(1, d_ff))


def _ple_kernel(h_ref, g_ref, wg_ref, p_ref, wp_ref, o_ref, xn_ref, *, tn):
    j = pl.program_id(1)

    @pl.when(j == 0)
    def _():
        xn_ref[...] = _rms(h_ref[...], g_ref[...]).astype(BF16)

    gate = _dot(xn_ref[...], wg_ref[...])
    proj = _dot(p_ref[...].astype(BF16), wp_ref[...])
    res = h_ref[:, pl.ds(pl.multiple_of(j * tn, tn), tn)]
    o_ref[...] = res + proj * _sigmoid(gate)


def _ple(h, gain, w_gate, p, w_proj, layer, *, tm, tn):
    m, k = h.shape
    n = w_gate.shape[2]
    kp = p.shape[2]
    tm, tn = min(tm, m), min(tn, n)
    assert m % tm == 0 and n % tn == 0 and n == k
    return pl.pallas_call(
        functools.partial(_ple_kernel, tn=tn),
        grid=(m // tm, n // tn),
        in_specs=[
            pl.BlockSpec((tm, k), lambda i, j: (i, 0)),
            pl.BlockSpec((1, k), lambda i, j: (0, 0)),
            _wspec(k, tn, layer, lambda i, j: j),
            pl.BlockSpec((None, tm, kp), lambda i, j: (layer, i, 0)),
            _wspec(kp, tn, layer, lambda i, j: j),
        ],
        out_specs=pl.BlockSpec((tm, tn), lambda i, j: (i, j)),
        out_shape=jax.ShapeDtypeStruct((m, n), F32),
        scratch_shapes=[pltpu.VMEM((tm, k), BF16)],
        compiler_params=_params("parallel", "arbitrary"),
        name="ple",
    )(h, gain.reshape(1, k), w_gate, p, w_proj)


def _rope_table_kernel(pos_ref, inv_ref, c_ref, sa_ref, sb_ref):
    ang = pos_ref[...].astype(F32) * inv_ref[...]
    lane = lax.broadcasted_iota(jnp.int32, ang.shape, 1)
    c = jnp.cos(ang)
    s = jnp.sin(ang)
    half = QK_ROPE // 2
    c_ref[...] = jnp.where(lane < QK_ROPE, c, 0.0)
    sa_ref[...] = jnp.where(lane < half, -s, 0.0)
    sb_ref[...] = jnp.where((lane >= half) & (lane < QK_ROPE), s, 0.0)


def _rope_tables(positions, *, tm):
    m = positions.size
    tm = min(tm, m)
    half = QK_ROPE // 2
    inv_freq = ROPE_THETA ** (-jnp.arange(0, QK_ROPE, 2, dtype=F32) / QK_ROPE)
    inv_row = jnp.concatenate([inv_freq, inv_freq, jnp.zeros((LANES - 2 * half,), F32)]).reshape(1, LANES)
    out = jax.ShapeDtypeStruct((m, LANES), F32)
    spec = pl.BlockSpec((tm, LANES), lambda i: (i, 0))
    return pl.pallas_call(
        _rope_table_kernel,
        grid=(m // tm,),
        in_specs=[pl.BlockSpec((tm, 1), lambda i: (i, 0)), pl.BlockSpec((1, LANES), lambda i: (0, 0))],
        out_specs=[spec, spec, spec],
        out_shape=[out, out, out],
        compiler_params=_params("parallel"),
        name="rope_tables",
    )(positions.reshape(m, 1), inv_row)


def _mla_proj_kernel(h_ref, g_ref, wa_ref, qn_ref, kvn_ref, wuq_ref, wukv_ref, gq_ref, gk_ref,
                     c_ref, sa_ref, sb_ref, q_out, k_out, v_out):
    xn = _rms(h_ref[...], g_ref[...]).astype(BF16)
    c = _dot(xn, wa_ref[...])
    cq = _rms(c[:, :Q_LORA], qn_ref[...]).astype(BF16)
    ckv = _rms(c[:, Q_LORA:Q_LORA + KV_LORA], kvn_ref[...]).astype(BF16)
    k_pe = c[:, Q_LORA + KV_LORA:]
    cos_t, sin_a, sin_b = c_ref[...], sa_ref[...], sb_ref[...]
    half = QK_ROPE // 2

    def rope(r):
        return r * cos_t + pltpu.roll(r, LANES - half, 1) * sin_a + pltpu.roll(r, half, 1) * sin_b

    gq = gq_ref[...]
    gk = gk_ref[...]
    k_pe_sq = jnp.sum(k_pe * k_pe, axis=-1, keepdims=True)
    k_rope = rope(k_pe * gk[:, QK_NOPE:])

    for hd in range(MLA_HEADS):
        lo = hd * HEAD_PAD
        mid = lo + QK_NOPE
        hi = lo + HEAD_PAD
        qh = _dot(cq, wuq_ref[:, lo:hi])
        inv = lax.rsqrt(jnp.sum(qh * qh, axis=-1, keepdims=True) * (1.0 / QK_HEAD) + RMS_EPS)
        qh = qh * inv * gq
        q_out[:, lo:mid] = qh[:, :QK_NOPE].astype(BF16)
        q_out[:, mid:hi] = rope(qh[:, QK_NOPE:]).astype(BF16)

        kvh = _dot(ckv, wukv_ref[:, lo:hi])
        kn = kvh[:, :QK_NOPE]
        inv_k = lax.rsqrt((jnp.sum(kn * kn, axis=-1, keepdims=True) + k_pe_sq) * (1.0 / QK_HEAD) + RMS_EPS)
        k_out[:, lo:mid] = (kn * inv_k * gk[:, :QK_NOPE]).astype(BF16)
        k_out[:, mid:hi] = (k_rope * inv_k).astype(BF16)
        v_out[:, hd * V_HEAD:(hd + 1) * V_HEAD] = kvh[:, QK_NOPE:].astype(BF16)


def _mla_proj(h, gain, w_a, q_norm, w_uq_pad, kv_norm, w_ukv, layer, gq_pad, gk_pad, tables, *, tm):
    m, k = h.shape
    tm = min(tm, m)
    assert m % tm == 0
    a_dim = w_a.shape[2]
    nq = MLA_HEADS * HEAD_PAD
    const = lambda i: (0, 0)
    row = lambda i: (i, 0)
    return pl.pallas_call(
        _mla_proj_kernel,
        grid=(m // tm,),
        in_specs=[
            pl.BlockSpec((tm, k), row),
            pl.BlockSpec((1, k), const),
            _wspec(k, a_dim, layer, lambda i: 0),
            pl.BlockSpec((1, Q_LORA), const),
            pl.BlockSpec((1, KV_LORA), const),
            _wspec(Q_LORA, nq, layer, lambda i: 0),
            _wspec(KV_LORA, nq, layer, lambda i: 0),
            pl.BlockSpec((1, HEAD_PAD), const),
            pl.BlockSpec((1, HEAD_PAD), const),
            pl.BlockSpec((tm, LANES), row),
            pl.BlockSpec((tm, LANES), row),
            pl.BlockSpec((tm, LANES), row),
        ],
        out_specs=[pl.BlockSpec((tm, nq), row), pl.BlockSpec((tm, nq), row),
                   pl.BlockSpec((tm, MLA_HEADS * V_HEAD), row)],
        out_shape=[jax.ShapeDtypeStruct((m, nq), BF16), jax.ShapeDtypeStruct((m, nq), BF16),
                   jax.ShapeDtypeStruct((m, MLA_HEADS * V_HEAD), BF16)],
        compiler_params=_params("parallel"),
        name="mla_proj",
    )(h, gain.reshape(1, k), w_a, q_norm.reshape(1, Q_LORA), kv_norm.reshape(1, KV_LORA),
      w_uq_pad, w_ukv, gq_pad, gk_pad, *tables)


def _attn_kernel(q_ref, k_ref, v_ref, o_ref, *, blk):
    seq = q_ref.shape[0]
    row = lax.broadcasted_iota(jnp.int32, (blk, blk), 0)
    col = lax.broadcasted_iota(jnp.int32, (blk, blk), 1)
    causal = col <= row
    for qi in range(seq // blk):
        q = q_ref[qi * blk:(qi + 1) * blk, :]
        m_run = jnp.full((blk, 1), -jnp.inf, F32)
        l_run = jnp.zeros((blk, 1), F32)
        acc = jnp.zeros((blk, V_HEAD), F32)
        for ki in range(qi + 1):
            s = _dot_nt(q, k_ref[ki * blk:(ki + 1) * blk, :])
            if ki == qi:
                s = jnp.where(causal, s, -jnp.inf)
            m_new = jnp.maximum(m_run, jnp.max(s, axis=-1, keepdims=True))
            p = jnp.exp2(s - m_new)
            alpha = jnp.exp2(m_run - m_new)
            l_run = alpha * l_run + jnp.sum(p, axis=-1, keepdims=True)
            acc = alpha * acc + _dot(p.astype(BF16), v_ref[ki * blk:(ki + 1) * blk, :])
            m_run = m_new
        o_ref[qi * blk:(qi + 1) * blk, :] = (acc / l_run).astype(o_ref.dtype)


def _attention(q, k, v, *, batch, seq, blk):
    m = q.shape[0]
    blk = min(blk, seq)
    assert seq % blk == 0
    return pl.pallas_call(
        functools.partial(_attn_kernel, blk=blk),
        grid=(batch, MLA_HEADS),
        in_specs=[
            pl.BlockSpec((seq, HEAD_PAD), lambda b, h: (b, h)),
            pl.BlockSpec((seq, HEAD_PAD), lambda b, h: (b, h)),
            pl.BlockSpec((seq, V_HEAD), lambda b, h: (b, h)),
        ],
        out_specs=pl.BlockSpec((seq, V_HEAD), lambda b, h: (b, h)),
        out_shape=jax.ShapeDtypeStruct((m, MLA_HEADS * V_HEAD), BF16),
        compiler_params=_params("parallel", "parallel"),
        name="mla_attention",
    )(q, k, v)


def _hgrn_prep(q_ref, f_ref, lb, tri, qs_ref, ks_ref, bs_ref, qi_ref, kd_ref, eb_ref, *, chunks):
    width = lb.shape[1]
    for c in range(chunks):
        rows = slice(c * CHUNK, (c + 1) * CHUNK)
        ff = f_ref[rows, :]
        f = lb + (1.0 - lb) * _sigmoid(ff)
        log_f = jnp.log2(f)
        k = (1.0 - lb) * _sigmoid(-ff)
        q_raw = q_ref[rows, :]
        q = q_raw * _sigmoid(q_raw)
        hi = log_f.astype(BF16)
        r1 = log_f - hi.astype(F32)
        mid = r1.astype(BF16)
        lo = (r1 - mid.astype(F32)).astype(BF16)
        parts = _dot(tri, jnp.concatenate([hi, mid, lo], axis=1))
        b = parts[:, :width] + parts[:, width:2 * width] + parts[:, 2 * width:]
        b_last = b[CHUNK - 1:CHUNK, :]
        qs_ref[rows, :] = q
        ks_ref[rows, :] = k
        bs_ref[rows, :] = b
        qi_ref[rows, :] = (q * jnp.exp2(b)).astype(BF16)
        kd_ref[rows, :] = (k * jnp.exp2(b_last - b)).astype(BF16)
        eb_ref[c] = jnp.exp2(b_last)


def _hgrn_intra(q, k, b):
    lane = lax.broadcasted_iota(jnp.int32, (HALF, CHUNK), 1)
    sub_t = lax.broadcasted_iota(jnp.int32, (HALF, HGRN_DK), 0)
    srow = lax.broadcasted_iota(jnp.int32, (CHUNK, HGRN_DK), 0)
    a_rows = []
    for blk in range(CHUNK // SUB):
        r0 = blk * SUB
        halves = []
        for hf in range(2):
            t0 = r0 + hf * HALF
            q_t = q[t0:t0 + HALF, :]
            b_t = b[t0:t0 + HALF, :]
            a_half = jnp.zeros((HALF, CHUNK), F32)
            for sl in range((hf + 1) * HALF):
                s = r0 + sl
                diff = b_t - b[s:s + 1, :]
                if sl >= hf * HALF:
                    diff = jnp.where(sub_t >= sl - hf * HALF, diff, -jnp.inf)
                col = jnp.sum(q_t * k[s:s + 1, :] * jnp.exp2(diff), axis=-1, keepdims=True)
                a_half = jnp.where(lane == s, col, a_half)
            halves.append(a_half)
        a_blk = jnp.concatenate(halves, axis=0)
        if blk > 0:
            ref_b = b[r0 - 1:r0, :]
            q_f = (q[r0:r0 + SUB, :] * jnp.exp2(b[r0:r0 + SUB, :] - ref_b)).astype(BF16)
            k_f = jnp.where(srow < r0, k * jnp.exp2(jnp.minimum(ref_b - b, 0.0)), 0.0).astype(BF16)
            a_blk = a_blk + _dot_nt(q_f, k_f)
        a_rows.append(a_blk)
    return jnp.concatenate(a_rows, axis=0)


def _hgrn_kernel(q_ref, f_ref, v_ref, g_ref, lb_ref, on_ref, o_ref,
                 st_ref, qs_ref, ks_ref, bs_ref, qi_ref, kd_ref, eb_ref, a_ref, *, heads, chunks):
    @pl.when(pl.program_id(2) == 0)
    def _():
        st_ref[...] = jnp.zeros_like(st_ref)

    r = lax.broadcasted_iota(jnp.int32, (CHUNK, CHUNK), 0)
    c = lax.broadcasted_iota(jnp.int32, (CHUNK, CHUNK), 1)
    tri = (c <= r).astype(BF16)
    _hgrn_prep(q_ref, f_ref, lb_ref[...], tri, qs_ref, ks_ref, bs_ref, qi_ref, kd_ref, eb_ref, chunks=chunks)

    def build_a(ci, slot):
        rows = pl.ds(pl.multiple_of(ci * CHUNK, CHUNK), CHUNK)
        for hd in range(heads):
            cols = slice(hd * HGRN_DK, (hd + 1) * HGRN_DK)
            a_ref[slot, hd] = _hgrn_intra(qs_ref[rows, cols], ks_ref[rows, cols], bs_ref[rows, cols]).astype(BF16)

    def finish(ci, slot):
        rows = pl.ds(pl.multiple_of(ci * CHUNK, CHUNK), CHUNK)
        for hd in range(heads):
            cols = slice(hd * HGRN_DK, (hd + 1) * HGRN_DK)
            v16 = v_ref[rows, cols].astype(BF16)
            st = st_ref[hd]
            o = _dot_nt(qi_ref[rows, cols], st.astype(BF16)) + _dot(a_ref[slot, hd], v16)
            st_ref[hd] = st * eb_ref[ci, :, cols] + _dot_tn(v16, kd_ref[rows, cols])
            o = o * lax.rsqrt(jnp.mean(o * o, axis=-1, keepdims=True) + RMS_EPS) * on_ref[:, cols]
            gate = g_ref[rows, cols]
            o_ref[rows, cols] = (o * (gate * _sigmoid(gate))).astype(o_ref.dtype)

    build_a(0, 0)

    def body(ci, carry):
        slot = ci % 2
        finish(ci, slot)
        build_a(jnp.minimum(ci + 1, chunks - 1), 1 - slot)
        return carry

    lax.fori_loop(0, chunks, body, 0)


def _hgrn_recurrence(z, lb, o_gain, *, batch, seq, heads, tb):
    m = z.shape[0]
    fdim = HGRN_HEADS * HGRN_DK
    tb = min(tb, seq)
    assert seq % tb == 0 and tb % CHUNK == 0 and HGRN_HEADS % heads == 0
    width = heads * HGRN_DK
    groups = HGRN_HEADS // heads
    nt = seq // tb
    chunks = tb // CHUNK

    def zspec(part):
        return pl.BlockSpec((tb, width), lambda b, g, t: (b * nt + t, part * groups + g))

    return pl.pallas_call(
        functools.partial(_hgrn_kernel, heads=heads, chunks=chunks),
        grid=(batch, groups, nt),
        in_specs=[zspec(0), zspec(1), zspec(2), zspec(3),
                  pl.BlockSpec((1, width), lambda b, g, t: (0, g)),
                  pl.BlockSpec((1, width), lambda b, g, t: (0, g))],
        out_specs=pl.BlockSpec((tb, width), lambda b, g, t: (b * nt + t, g)),
        out_shape=jax.ShapeDtypeStruct((m, fdim), BF16),
        scratch_shapes=[
            pltpu.VMEM((heads, HGRN_DV, HGRN_DK), F32),
            pltpu.VMEM((tb, width), F32),
            pltpu.VMEM((tb, width), F32),
            pltpu.VMEM((tb, width), F32),
            pltpu.VMEM((tb, width), BF16),
            pltpu.VMEM((tb, width), BF16),
            pltpu.VMEM((chunks, 1, width), F32),
            pltpu.VMEM((2, heads, CHUNK, CHUNK), BF16),
        ],
        compiler_params=_params("parallel", "parallel", "arbitrary"),
        name="hgrn_recurrence",
    )(z, z, z, z, lb.reshape(1, fdim), o_gain.reshape(1, fdim))


def _mla_layer(h, tables, layer, mix_gain, w_a_pad, q_norm, w_uq_pad, kv_norm, w_ukv, gain_q, gain_k, w_o,
               *, batch, seq):
    pad = HEAD_PAD - QK_HEAD
    gq_pad = jnp.pad(gain_q * (math.log2(math.e) / math.sqrt(QK_HEAD)), (0, pad)).reshape(1, HEAD_PAD)
    gk_pad = jnp.pad(gain_k, (0, pad)).reshape(1, HEAD_PAD)
    q, k, v = _mla_proj(h, mix_gain, w_a_pad, q_norm, w_uq_pad, kv_norm, w_ukv, layer, gq_pad, gk_pad, tables, tm=256)
    o = _attention(q, k, v, batch=batch, seq=seq, blk=512)
    return _matmul(o, w_o, layer, residual=h, out_dtype=F32, tm=1024, tn=1024, name="mla_out")


def _hgrn_layer(h, layer, mix_gain, lb, w_in, o_gain, w_o, *, batch, seq):
    z = _matmul(h, w_in, layer, gain=mix_gain, out_dtype=F32, tm=1024, tn=1024, name="hgrn_in")
    o = _hgrn_recurrence(z, lb, o_gain, batch=batch, seq=seq, heads=4, tb=1024)
    return _matmul(o, w_o, layer, residual=h, out_dtype=F32, tm=1024, tn=1024, name="hgrn_out")


def kernel(x, p, positions, mix_norm, ffn_norm, ple_norm, mla_w_a, mla_q_norm, mla_w_uq, mla_kv_norm, mla_w_ukv, mla_qk_gain_q, mla_qk_gain_k, mla_w_o, hgrn_lb_logits, hgrn_w_in, hgrn_o_norm, hgrn_w_o, ffn_w_in, ffn_conv_w, ffn_conv_b, ffn_w_down, ple_w_proj, ple_w_gate):
    batch, seq, d_model = x.shape
    depth = p.shape[0]
    m = batch * seq
    tables = _rope_tables(positions, tm=1024)
    sm = jax.nn.softmax(hgrn_lb_logits.astype(F32), axis=0)
    lower_bounds = jnp.cumsum(sm, axis=0) - sm[0]

    n_mla = mla_w_a.shape[0]
    w_a_pad = jnp.pad(mla_w_a, ((0, 0), (0, 0), (0, LANES - QK_ROPE))).astype(BF16)
    w_uq_pad = jnp.pad(mla_w_uq.reshape(n_mla, Q_LORA, MLA_HEADS, QK_HEAD),
                       ((0, 0), (0, 0), (0, 0), (0, HEAD_PAD - QK_HEAD)))
    w_uq_pad = w_uq_pad.reshape(n_mla, Q_LORA, MLA_HEADS * HEAD_PAD).astype(BF16)
    w_ukv = mla_w_ukv.astype(BF16)
    w_mla_o = mla_w_o.astype(BF16)
    w_hgrn_in = hgrn_w_in.astype(BF16)
    w_hgrn_o = hgrn_w_o.astype(BF16)
    w_ffn_in = ffn_w_in.astype(BF16)
    w_ffn_down = ffn_w_down.astype(BF16)
    w_ple_gate = ple_w_gate.astype(BF16)
    w_ple_proj = ple_w_proj.astype(BF16)
    p_rows = p.reshape(depth, m, -1)

    h = x.reshape(m, d_model)
    for i in range(depth):
        j = i // 2
        if i % 2 == 0:
            h = _mla_layer(h, tables, j, mix_norm[i], w_a_pad, mla_q_norm[j], w_uq_pad, mla_kv_norm[j], w_ukv,
                           mla_qk_gain_q[j], mla_qk_gain_k[j], w_mla_o, batch=batch, seq=seq)
        else:
            h = _hgrn_layer(h, j, mix_norm[i], lower_bounds[i], w_hgrn_in, hgrn_o_norm[j], w_hgrn_o,
                            batch=batch, seq=seq)
        act = _ffn_in(h, ffn_norm[i], w_ffn_in, i, ffn_conv_w[i], ffn_conv_b[i], seq=seq, tm=1024, tn=512)
        h = _matmul(act, w_ffn_down, i, residual=h, out_dtype=F32, tm=1024, tn=512, name="ffn_down")
        h = _ple(h, ple_norm[i], w_ple_gate, p_rows, w_ple_proj, i, tm=1024, tn=1024)
    return h.reshape(batch, seq, d_model)
```

```python
import functools
import math

import jax
import jax.numpy as jnp
from jax import lax
from jax.experimental import pallas as pl
from jax.experimental.pallas import tpu as pltpu

F32 = jnp.float32
BF16 = jnp.bfloat16

RMS_EPS = 1e-6
ROPE_THETA = 10000.0
LOG2_E = math.log2(math.e)

LANES = 128
BF16_SUBLANES = 16
VMEM_LIMIT_BYTES = 56 * 1024 * 1024

MLA_HEADS = 16
Q_LORA = 512
KV_LORA = 512
QK_NOPE = 128
QK_ROPE = 64
QK_HEAD = QK_NOPE + QK_ROPE
V_HEAD = 128
HEAD_PAD = 256

HGRN_HEADS = 16
HGRN_DK = 128
HGRN_DV = 128
CHUNK = 64
SUB = 16
HALF = SUB // 2

CONV_W = 3
CONV_HALO = BF16_SUBLANES


def _params(*sem):
    return pltpu.CompilerParams(dimension_semantics=sem, vmem_limit_bytes=VMEM_LIMIT_BYTES)


def _rms(x, g):
    return x * lax.rsqrt(jnp.mean(x * x, axis=-1, keepdims=True) + RMS_EPS) * g


def _sigmoid(x):
    return 1.0 / (1.0 + jnp.exp(-x))


def _dot(a, b):
    return jnp.dot(a, b, preferred_element_type=F32)


def _dot_nt(a, b):
    return lax.dot_general(a, b, (((1,), (1,)), ((), ())), preferred_element_type=F32)


def _dot_tn(a, b):
    return lax.dot_general(a, b, (((0,), (0,)), ((), ())), preferred_element_type=F32)


def _wspec(k, tn, layer, col):
    if layer is None:
        return pl.BlockSpec((k, tn), lambda *g: (0, col(*g)))
    return pl.BlockSpec((None, k, tn), lambda *g: (layer, 0, col(*g)))


def _pallas(body, ins, *, grid, in_specs, out_specs, out_shape, sem, name, scratch_shapes=(), casts=()):
    single = not isinstance(out_shape, (list, tuple))
    out_specs = [out_specs] if single else list(out_specs)
    out_shape = [out_shape] if single else list(out_shape)
    n_in, n_out, n_cast = len(ins), len(out_shape), len(casts)
    steps = math.prod(grid)

    def step_of(*g):
        idx = 0
        for size, pos in zip(grid, g):
            idx = idx * size + pos
        return idx

    c_in, c_in_specs, c_out_specs, c_out_shape = [], [], [], []
    for stack, layer in casts:
        _, rows, cols = stack.shape
        rb = rows // steps
        assert rows % steps == 0 and rb % BF16_SUBLANES == 0
        c_in.append(stack)
        c_in_specs.append(pl.BlockSpec((None, rb, cols), lambda *g, layer=layer: (layer, step_of(*g), 0)))
        c_out_specs.append(pl.BlockSpec((rb, cols), lambda *g: (step_of(*g), 0)))
        c_out_shape.append(jax.ShapeDtypeStruct((rows, cols), BF16))

    def kernel(*refs):
        outs_at = n_in + n_cast
        copies_at = outs_at + n_out
        for src, dst in zip(refs[n_in:outs_at], refs[copies_at:copies_at + n_cast]):
            dst[...] = src[...].astype(BF16)
        body(*refs[:n_in], *refs[outs_at:copies_at], *refs[copies_at + n_cast:])

    res = pl.pallas_call(
        kernel,
        grid=grid,
        in_specs=[*in_specs, *c_in_specs],
        out_specs=[*out_specs, *c_out_specs],
        out_shape=[*out_shape, *c_out_shape],
        scratch_shapes=list(scratch_shapes),
        compiler_params=_params(*sem),
        name=name,
    )(*ins, *c_in)
    outs = res[0] if single else list(res[:n_out])
    return outs, list(res[n_out:])


def _matmul_kernel(*refs, norm, residual):
    it = iter(refs)
    x_ref = next(it)
    g_ref = next(it) if norm else None
    w_ref = next(it)
    r_ref = next(it) if residual else None
    o_ref = next(it)
    xn_ref = next(it) if norm else None

    if norm:
        @pl.when(pl.program_id(1) == 0)
        def _():
            xn_ref[...] = _rms(x_ref[...], g_ref[...]).astype(BF16)
        acc = _dot(xn_ref[...], w_ref[...])
    else:
        acc = _dot(x_ref[...], w_ref[...])
    if residual:
        acc = r_ref[...] + acc
    o_ref[...] = acc.astype(o_ref.dtype)


def _matmul(x, w, layer, *, gain=None, residual=None, out_dtype, tm, tn, name, casts=()):
    m, k = x.shape
    n = w.shape[-1]
    tm, tn = min(tm, m), min(tn, n)
    assert m % tm == 0 and n % tn == 0
    norm = gain is not None
    ins = [x]
    specs = [pl.BlockSpec((tm, k), lambda i, j: (i, 0))]
    if norm:
        ins.append(gain.reshape(1, k))
        specs.append(pl.BlockSpec((1, k), lambda i, j: (0, 0)))
    ins.append(w)
    specs.append(_wspec(k, tn, layer, lambda i, j: j))
    if residual is not None:
        ins.append(residual)
        specs.append(pl.BlockSpec((tm, tn), lambda i, j: (i, j)))
    return _pallas(
        functools.partial(_matmul_kernel, norm=norm, residual=residual is not None), ins,
        grid=(m // tm, n // tn),
        in_specs=specs,
        out_specs=pl.BlockSpec((tm, tn), lambda i, j: (i, j)),
        out_shape=jax.ShapeDtypeStruct((m, n), out_dtype),
        scratch_shapes=[pltpu.VMEM((tm, k), BF16)] if norm else [],
        sem=("parallel", "arbitrary"),
        name=name,
        casts=casts,
    )


def _ffn_in_kernel(x_ref, halo_ref, g_ref, wg_ref, wu_ref, cw_ref, cb_ref, o_ref, xn_ref, gate_ref,
                   *, tm, tiles_per_seq):
    i = pl.program_id(0)

    @pl.when(pl.program_id(1) == 0)
    def _():
        g = g_ref[...]
        keep = jnp.where(i % tiles_per_seq != 0, 1.0, 0.0)
        xn_ref[:CONV_HALO, :] = (_rms(halo_ref[...], g) * keep).astype(BF16)
        xn_ref[CONV_HALO:, :] = _rms(x_ref[...], g).astype(BF16)

    gate_ref[...] = _dot(xn_ref[...], wg_ref[...])
    up = _dot(xn_ref[CONV_HALO:, :], wu_ref[...])
    cw = cw_ref[...]
    conv = cb_ref[...] + gate_ref[pl.ds(CONV_HALO - 2, tm), :] * cw[0:1, :]
    conv = conv + gate_ref[pl.ds(CONV_HALO - 1, tm), :] * cw[1:2, :]
    conv = conv + gate_ref[pl.ds(CONV_HALO, tm), :] * cw[2:3, :]
    o_ref[...] = (conv * _sigmoid(conv) * up).astype(o_ref.dtype)


def _ffn_in(h, gain, w_in, layer, conv_w, conv_b, *, seq, tm, tn, casts=()):
    m, k = h.shape
    d_ff = w_in.shape[-1] // 2
    tm = min(tm, seq)
    assert seq % tm == 0 and d_ff % tn == 0 and tm % CONV_HALO == 0
    nj = d_ff // tn
    halo_blocks = tm // CONV_HALO
    return _pallas(
        functools.partial(_ffn_in_kernel, tm=tm, tiles_per_seq=seq // tm),
        (h, h, gain.reshape(1, k), w_in, w_in, conv_w, conv_b.reshape(1, d_ff)),
        grid=(m // tm, nj),
        in_specs=[
            pl.BlockSpec((tm, k), lambda i, j: (i, 0)),
            pl.BlockSpec((CONV_HALO, k), lambda i, j: (jnp.maximum(i * halo_blocks - 1, 0), 0)),
            pl.BlockSpec((1, k), lambda i, j: (0, 0)),
            _wspec(k, tn, layer, lambda i, j: j),
            _wspec(k, tn, layer, lambda i, j: j + nj),
            pl.BlockSpec((CONV_W, tn), lambda i, j: (0, j)),
            pl.BlockSpec((1, tn), lambda i, j: (0, j)),
        ],
        out_specs=pl.BlockSpec((tm, tn), lambda i, j: (i, j)),
        out_shape=jax.ShapeDtypeStruct((m, d_ff), BF16),
        scratch_shapes=[pltpu.VMEM((tm + CONV_HALO, k), BF16), pltpu.VMEM((tm + CONV_HALO, tn), F32)],
        sem=("parallel", "arbitrary"),
        name="ffn_in",
        casts=casts,
    )


def _ple_kernel(h_ref, g_ref, wg_ref, p_ref, wp_ref, o_ref):
    h = h_ref[...]
    proj = _dot(p_ref[...].astype(BF16), wp_ref[...])
    gate = _dot(_rms(h, g_ref[...]).astype(BF16), wg_ref[...])
    o_ref[...] = h + proj * _sigmoid(gate)


def _ple(h, gain, w_gate, p, w_proj, layer, *, tm):
    m, k = h.shape
    n = w_gate.shape[-1]
    kp = p.shape[2]
    tm = min(tm, m)
    assert m % tm == 0 and n == k
    gate_layer = layer if w_gate.ndim == 3 else None
    return pl.pallas_call(
        _ple_kernel,
        grid=(m // tm,),
        in_specs=[
            pl.BlockSpec((tm, k), lambda i: (i, 0)),
            pl.BlockSpec((1, k), lambda i: (0, 0)),
            _wspec(k, n, gate_layer, lambda i: 0),
            pl.BlockSpec((None, tm, kp), lambda i: (layer, i, 0)),
            _wspec(kp, n, layer, lambda i: 0),
        ],
        out_specs=pl.BlockSpec((tm, n), lambda i: (i, 0)),
        out_shape=jax.ShapeDtypeStruct((m, n), F32),
        compiler_params=_params("parallel"),
        name="ple",
    )(h, gain.reshape(1, k), w_gate, p, w_proj)


def _rope_table_kernel(pos_ref, inv_ref, c_ref, sa_ref, sb_ref):
    ang = pos_ref[...].astype(F32) * inv_ref[...]
    lane = lax.broadcasted_iota(jnp.int32, ang.shape, 1)
    c = jnp.cos(ang)
    s = jnp.sin(ang)
    half = QK_ROPE // 2
    c_ref[...] = jnp.where(lane < QK_ROPE, c, 0.0)
    sa_ref[...] = jnp.where(lane < half, -s, 0.0)
    sb_ref[...] = jnp.where((lane >= half) & (lane < QK_ROPE), s, 0.0)


def _rope_tables(positions, *, tm):
    m = positions.size
    tm = min(tm, m)
    half = QK_ROPE // 2
    inv_freq = ROPE_THETA ** (-jnp.arange(0, QK_ROPE, 2, dtype=F32) / QK_ROPE)
    inv_row = jnp.concatenate([inv_freq, inv_freq, jnp.zeros((LANES - 2 * half,), F32)]).reshape(1, LANES)
    out = jax.ShapeDtypeStruct((m, LANES), F32)
    spec = pl.BlockSpec((tm, LANES), lambda i: (i, 0))
    return pl.pallas_call(
        _rope_table_kernel,
        grid=(m // tm,),
        in_specs=[pl.BlockSpec((tm, 1), lambda i: (i, 0)), pl.BlockSpec((1, LANES), lambda i: (0, 0))],
        out_specs=[spec, spec, spec],
        out_shape=[out, out, out],
        compiler_params=_params("parallel"),
        name="rope_tables",
    )(positions.reshape(m, 1), inv_row)


def _mla_proj_kernel(h_ref, g_ref, wa_ref, qn_ref, kvn_ref, wuq_ref, wukv_ref, gq_ref, gk_ref,
                     c_ref, sa_ref, sb_ref, q_out, k_out, v_out):
    xn = _rms(h_ref[...], g_ref[...]).astype(BF16)
    c = _dot(xn, wa_ref[...])
    cq = _rms(c[:, :Q_LORA], qn_ref[...]).astype(BF16)
    ckv = _rms(c[:, Q_LORA:Q_LORA + KV_LORA], kvn_ref[...]).astype(BF16)
    k_pe = c[:, Q_LORA + KV_LORA:]
    cos_t, sin_a, sin_b = c_ref[...], sa_ref[...], sb_ref[...]
    half = QK_ROPE // 2

    def rope(r):
        return r * cos_t + pltpu.roll(r, LANES - half, 1) * sin_a + pltpu.roll(r, half, 1) * sin_b

    gq = gq_ref[...]
    gk = gk_ref[...]
    k_pe_sq = jnp.sum(k_pe * k_pe, axis=-1, keepdims=True)
    k_rope = rope(k_pe * gk[:, QK_NOPE:])

    for hd in range(MLA_HEADS):
        lo = hd * HEAD_PAD
        mid = lo + QK_NOPE
        hi = lo + HEAD_PAD
        qh = _dot(cq, wuq_ref[:, lo:hi])
        inv = lax.rsqrt(jnp.sum(qh * qh, axis=-1, keepdims=True) * (1.0 / QK_HEAD) + RMS_EPS)
        qh = qh * inv * gq
        q_out[:, lo:mid] = qh[:, :QK_NOPE].astype(BF16)
        q_out[:, mid:hi] = rope(qh[:, QK_NOPE:]).astype(BF16)

        kvh = _dot(ckv, wukv_ref[:, lo:hi])
        kn = kvh[:, :QK_NOPE]
        inv_k = lax.rsqrt((jnp.sum(kn * kn, axis=-1, keepdims=True) + k_pe_sq) * (1.0 / QK_HEAD) + RMS_EPS)
        k_out[:, lo:mid] = (kn * inv_k * gk[:, :QK_NOPE]).astype(BF16)
        k_out[:, mid:hi] = (k_rope * inv_k).astype(BF16)
        v_out[:, hd * V_HEAD:(hd + 1) * V_HEAD] = kvh[:, QK_NOPE:].astype(BF16)


def _mla_proj(h, gain, w_a, q_norm, w_uq_pad, kv_norm, w_ukv, layer, gq_pad, gk_pad, tables, *, tm):
    m, k = h.shape
    tm = min(tm, m)
    assert m % tm == 0
    a_dim = w_a.shape[2]
    nq = MLA_HEADS * HEAD_PAD
    const = lambda i: (0, 0)
    row = lambda i: (i, 0)
    return pl.pallas_call(
        _mla_proj_kernel,
        grid=(m // tm,),
        in_specs=[
            pl.BlockSpec((tm, k), row),
            pl.BlockSpec((1, k), const),
            _wspec(k, a_dim, layer, lambda i: 0),
            pl.BlockSpec((1, Q_LORA), const),
            pl.BlockSpec((1, KV_LORA), const),
            _wspec(Q_LORA, nq, layer, lambda i: 0),
            _wspec(KV_LORA, nq, layer, lambda i: 0),
            pl.BlockSpec((1, HEAD_PAD), const),
            pl.BlockSpec((1, HEAD_PAD), const),
            pl.BlockSpec((tm, LANES), row),
            pl.BlockSpec((tm, LANES), row),
            pl.BlockSpec((tm, LANES), row),
        ],
        out_specs=[pl.BlockSpec((tm, nq), row), pl.BlockSpec((tm, nq), row),
                   pl.BlockSpec((tm, MLA_HEADS * V_HEAD), row)],
        out_shape=[jax.ShapeDtypeStruct((m, nq), BF16), jax.ShapeDtypeStruct((m, nq), BF16),
                   jax.ShapeDtypeStruct((m, MLA_HEADS * V_HEAD), BF16)],
        compiler_params=_params("parallel"),
        name="mla_proj",
    )(h, gain.reshape(1, k), w_a, q_norm.reshape(1, Q_LORA), kv_norm.reshape(1, KV_LORA),
      w_uq_pad, w_ukv, gq_pad, gk_pad, *tables)


def _attn_kernel(q_ref, k_ref, v_ref, o_ref, *, blk):
    seq = q_ref.shape[0]
    row = lax.broadcasted_iota(jnp.int32, (blk, blk), 0)
    col = lax.broadcasted_iota(jnp.int32, (blk, blk), 1)
    causal = col <= row
    for qi in range(seq // blk):
        q = q_ref[qi * blk:(qi + 1) * blk, :]
        m_run = jnp.full((blk, 1), -jnp.inf, F32)
        l_run = jnp.zeros((blk, 1), F32)
        acc = jnp.zeros((blk, V_HEAD), F32)
        for ki in range(qi + 1):
            s = _dot_nt(q, k_ref[ki * blk:(ki + 1) * blk, :])
            if ki == qi:
                s = jnp.where(causal, s, -jnp.inf)
            m_new = jnp.maximum(m_run, jnp.max(s, axis=-1, keepdims=True))
            p = jnp.exp2(s - m_new)
            alpha = jnp.exp2(m_run - m_new)
            l_run = alpha * l_run + jnp.sum(p, axis=-1, keepdims=True)
            acc = alpha * acc + _dot(p.astype(BF16), v_ref[ki * blk:(ki + 1) * blk, :])
            m_run = m_new
        o_ref[qi * blk:(qi + 1) * blk, :] = (acc / l_run).astype(o_ref.dtype)


def _attention(q, k, v, *, batch, seq, blk, casts=()):
    m = q.shape[0]
    blk = min(blk, seq)
    assert seq % blk == 0
    return _pallas(
        functools.partial(_attn_kernel, blk=blk), (q, k, v),
        grid=(batch, MLA_HEADS),
        in_specs=[
            pl.BlockSpec((seq, HEAD_PAD), lambda b, h: (b, h)),
            pl.BlockSpec((seq, HEAD_PAD), lambda b, h: (b, h)),
            pl.BlockSpec((seq, V_HEAD), lambda b, h: (b, h)),
        ],
        out_specs=pl.BlockSpec((seq, V_HEAD), lambda b, h: (b, h)),
        out_shape=jax.ShapeDtypeStruct((m, MLA_HEADS * V_HEAD), BF16),
        sem=("parallel", "parallel"),
        name="mla_attention",
        casts=casts,
    )


def _hgrn_prep(q_ref, f_ref, lb, tri, qs_ref, cs_ref, bs_ref, qi_ref, kd_ref, eb_ref, *, chunks):
    width = lb.shape[1]
    for c in range(chunks):
        rows = slice(c * CHUNK, (c + 1) * CHUNK)
        ff = f_ref[rows, :]
        e_abs = jnp.exp(-jnp.abs(ff))
        r_abs = 1.0 / (1.0 + e_abs)
        f = lb + (1.0 - lb) * jnp.where(ff >= 0.0, r_abs, e_abs * r_abs)
        log_f = jnp.log2(f)
        log_k = jnp.log2(1.0 - lb) - (jnp.maximum(ff, 0.0) * LOG2_E + jnp.log2(1.0 + e_abs))
        q_raw = q_ref[rows, :]
        q = q_raw * _sigmoid(q_raw)
        hi = log_f.astype(BF16)
        r1 = log_f - hi.astype(F32)
        mid = r1.astype(BF16)
        lo = (r1 - mid.astype(F32)).astype(BF16)
        parts = _dot(tri, jnp.concatenate([hi, mid, lo], axis=1))
        b = parts[:, :width] + parts[:, width:2 * width] + parts[:, 2 * width:]
        b_last = b[CHUNK - 1:CHUNK, :]
        c_dec = b - log_k
        qs_ref[rows, :] = q
        cs_ref[rows, :] = c_dec
        bs_ref[rows, :] = b
        qi_ref[rows, :] = (q * jnp.exp2(b)).astype(BF16)
        kd_ref[rows, :] = jnp.exp2(b_last - c_dec).astype(BF16)
        eb_ref[c] = jnp.exp2(b_last)


def _hgrn_intra(q, c_dec, b, ones):
    lane = lax.broadcasted_iota(jnp.int32, (HALF, HGRN_DK), 1)
    srow = lax.broadcasted_iota(jnp.int32, (CHUNK, HGRN_DK), 0)
    a_rows = []
    for blk in range(CHUNK // SUB):
        r0 = blk * SUB
        prods = []
        for hf in range(2):
            t0 = r0 + hf * HALF
            q_t = q[t0:t0 + HALF, :]
            b_t = b[t0:t0 + HALF, :]
            for sl in range((hf + 1) * HALF):
                prods.append(q_t * jnp.exp2(b_t - c_dec[r0 + sl:r0 + sl + 1, :]))
        sums = _dot(jnp.concatenate(prods, axis=0).astype(BF16), ones)
        halves = []
        piece = 0
        for hf in range(2):
            a_half = jnp.zeros((HALF, HGRN_DK), F32)
            for sl in range((hf + 1) * HALF):
                a_half = jnp.where(lane == r0 + sl, sums[piece * HALF:(piece + 1) * HALF, :], a_half)
                piece += 1
            halves.append(a_half)
        a_blk = jnp.concatenate(halves, axis=0)[:, :CHUNK]
        if blk > 0:
            ref_b = b[r0 - 1:r0, :]
            q_f = (q[r0:r0 + SUB, :] * jnp.exp2(b[r0:r0 + SUB, :] - ref_b)).astype(BF16)
            k_f = jnp.where(srow < r0, jnp.exp2(ref_b - c_dec), 0.0).astype(BF16)
            a_blk = a_blk + _dot_nt(q_f, k_f)
        a_rows.append(a_blk)
    t_idx = lax.broadcasted_iota(jnp.int32, (CHUNK, CHUNK), 0)
    s_idx = lax.broadcasted_iota(jnp.int32, (CHUNK, CHUNK), 1)
    return jnp.where(s_idx <= t_idx, jnp.concatenate(a_rows, axis=0), 0.0)


def _hgrn_kernel(q_ref, f_ref, v_ref, g_ref, lb_ref, on_ref, o_ref,
                 st_ref, qs_ref, cs_ref, bs_ref, qi_ref, kd_ref, eb_ref, a_ref, *, heads, chunks):
    @pl.when(pl.program_id(2) == 0)
    def _():
        st_ref[...] = jnp.zeros_like(st_ref)

    r = lax.broadcasted_iota(jnp.int32, (CHUNK, CHUNK), 0)
    c = lax.broadcasted_iota(jnp.int32, (CHUNK, CHUNK), 1)
    tri = (c <= r).astype(BF16)
    ones = jnp.ones((HGRN_DK, HGRN_DK), BF16)
    _hgrn_prep(q_ref, f_ref, lb_ref[...], tri, qs_ref, cs_ref, bs_ref, qi_ref, kd_ref, eb_ref, chunks=chunks)

    def build_a(ci, slot):
        rows = pl.ds(pl.multiple_of(ci * CHUNK, CHUNK), CHUNK)
        for hd in range(heads):
            cols = slice(hd * HGRN_DK, (hd + 1) * HGRN_DK)
            a_ref[slot, hd] = _hgrn_intra(qs_ref[rows, cols], cs_ref[rows, cols], bs_ref[rows, cols],
                                          ones).astype(BF16)

    def finish(ci, slot):
        rows = pl.ds(pl.multiple_of(ci * CHUNK, CHUNK), CHUNK)
        for hd in range(heads):
            cols = slice(hd * HGRN_DK, (hd + 1) * HGRN_DK)
            v16 = v_ref[rows, cols].astype(BF16)
            st = st_ref[hd]
            o = _dot_nt(qi_ref[rows, cols], st.astype(BF16)) + _dot(a_ref[slot, hd], v16)
            st_ref[hd] = st * eb_ref[ci, :, cols] + _dot_tn(v16, kd_ref[rows, cols])
            o = o * lax.rsqrt(jnp.mean(o * o, axis=-1, keepdims=True) + RMS_EPS) * on_ref[:, cols]
            gate = g_ref[rows, cols]
            o_ref[rows, cols] = (o * (gate * _sigmoid(gate))).astype(o_ref.dtype)

    build_a(0, 0)

    def body(ci, carry):
        slot = ci % 2
        finish(ci, slot)
        build_a(jnp.minimum(ci + 1, chunks - 1), 1 - slot)
        return carry

    lax.fori_loop(0, chunks, body, 0)


def _hgrn_recurrence(z, lb, o_gain, *, batch, seq, heads, tb, casts=()):
    m = z.shape[0]
    fdim = HGRN_HEADS * HGRN_DK
    tb = min(tb, seq)
    assert seq % tb == 0 and tb % CHUNK == 0 and HGRN_HEADS % heads == 0
    width = heads * HGRN_DK
    groups = HGRN_HEADS // heads
    nt = seq // tb
    chunks = tb // CHUNK

    def zspec(part):
        return pl.BlockSpec((tb, width), lambda b, g, t: (b * nt + t, part * groups + g))

    return _pallas(
        functools.partial(_hgrn_kernel, heads=heads, chunks=chunks),
        (z, z, z, z, lb.reshape(1, fdim), o_gain.reshape(1, fdim)),
        grid=(batch, groups, nt),
        in_specs=[zspec(0), zspec(1), zspec(2), zspec(3),
                  pl.BlockSpec((1, width), lambda b, g, t: (0, g)),
                  pl.BlockSpec((1, width), lambda b, g, t: (0, g))],
        out_specs=pl.BlockSpec((tb, width), lambda b, g, t: (b * nt + t, g)),
        out_shape=jax.ShapeDtypeStruct((m, fdim), BF16),
        scratch_shapes=[
            pltpu.VMEM((heads, HGRN_DV, HGRN_DK), F32),
            pltpu.VMEM((tb, width), F32),
            pltpu.VMEM((tb, width), F32),
            pltpu.VMEM((tb, width), F32),
            pltpu.VMEM((tb, width), BF16),
            pltpu.VMEM((tb, width), BF16),
            pltpu.VMEM((chunks, 1, width), F32),
            pltpu.VMEM((2, heads, CHUNK, CHUNK), BF16),
        ],
        sem=("parallel", "parallel", "arbitrary"),
        name="hgrn_recurrence",
        casts=casts,
    )


def _mla_layer(h, tables, layer, mix_gain, w_a_pad, q_norm, w_uq_pad, kv_norm, w_ukv, gain_q, gain_k, w_o,
               *, batch, seq, casts=()):
    pad = HEAD_PAD - QK_HEAD
    gq_pad = jnp.pad(gain_q * (LOG2_E / math.sqrt(QK_HEAD)), (0, pad)).reshape(1, HEAD_PAD)
    gk_pad = jnp.pad(gain_k, (0, pad)).reshape(1, HEAD_PAD)
    q, k, v = _mla_proj(h, mix_gain, w_a_pad, q_norm, w_uq_pad, kv_norm, w_ukv, layer, gq_pad, gk_pad, tables, tm=256)
    o, copies = _attention(q, k, v, batch=batch, seq=seq, blk=512, casts=casts)
    h, _ = _matmul(o, w_o, layer, residual=h, out_dtype=F32, tm=512, tn=2048, name="mla_out")
    return h, copies


def _hgrn_layer(h, layer, mix_gain, lb, w_in, o_gain, w_o, *, batch, seq, casts=()):
    z, _ = _matmul(h, w_in, None, gain=mix_gain, out_dtype=F32, tm=1024, tn=1024, name="hgrn_in")
    o, copies = _hgrn_recurrence(z, lb, o_gain, batch=batch, seq=seq, heads=4, tb=1024, casts=casts)
    h, _ = _matmul(o, w_o, layer, residual=h, out_dtype=F32, tm=512, tn=2048, name="hgrn_out")
    return h, copies


def kernel(x, p, positions, mix_norm, ffn_norm, ple_norm, mla_w_a, mla_q_norm, mla_w_uq, mla_kv_norm, mla_w_ukv, mla_qk_gain_q, mla_qk_gain_k, mla_w_o, hgrn_lb_logits, hgrn_w_in, hgrn_o_norm, hgrn_w_o, ffn_w_in, ffn_conv_w, ffn_conv_b, ffn_w_down, ple_w_proj, ple_w_gate):
    batch, seq, d_model = x.shape
    depth = p.shape[0]
    m = batch * seq
    tables = _rope_tables(positions, tm=1024)
    sm = jax.nn.softmax(hgrn_lb_logits.astype(F32), axis=0)
    lower_bounds = jnp.cumsum(sm, axis=0) - sm[0]

    n_mla = mla_w_a.shape[0]
    w_a_pad = jnp.pad(mla_w_a, ((0, 0), (0, 0), (0, LANES - QK_ROPE))).astype(BF16)
    w_uq_pad = jnp.pad(mla_w_uq.reshape(n_mla, Q_LORA, MLA_HEADS, QK_HEAD),
                       ((0, 0), (0, 0), (0, 0), (0, HEAD_PAD - QK_HEAD)))
    w_uq_pad = w_uq_pad.reshape(n_mla, Q_LORA, MLA_HEADS * HEAD_PAD).astype(BF16)
    w_ukv = mla_w_ukv.astype(BF16)
    w_mla_o = mla_w_o.astype(BF16)
    w_hgrn_o = hgrn_w_o.astype(BF16)
    w_ple_proj = ple_w_proj.astype(BF16)
    p_rows = p.reshape(depth, m, -1)

    h = x.reshape(m, d_model)
    w_hgrn_in = None
    for i in range(depth):
        j = i // 2
        if i % 2 == 0:
            h, (w_ffn_in,) = _mla_layer(h, tables, j, mix_norm[i], w_a_pad, mla_q_norm[j], w_uq_pad, mla_kv_norm[j],
                                        w_ukv, mla_qk_gain_q[j], mla_qk_gain_k[j], w_mla_o, batch=batch, seq=seq,
                                        casts=[(ffn_w_in, i)])
        else:
            h, (w_ffn_in,) = _hgrn_layer(h, j, mix_norm[i], lower_bounds[i], w_hgrn_in, hgrn_o_norm[j], w_hgrn_o,
                                         batch=batch, seq=seq, casts=[(ffn_w_in, i)])
        act, (w_ffn_down,) = _ffn_in(h, ffn_norm[i], w_ffn_in, None, ffn_conv_w[i], ffn_conv_b[i],
                                     seq=seq, tm=1024, tn=512, casts=[(ffn_w_down, i)])
        down_casts = [(ple_w_gate, i)]
        if i + 1 < depth and (i + 1) % 2 == 1:
            down_casts.append((hgrn_w_in, (i + 1) // 2))
        h, copies = _matmul(act, w_ffn_down, None, residual=h, out_dtype=F32, tm=1024, tn=512, name="ffn_down",
                            casts=down_casts)
        w_hgrn_in = copies[1] if len(copies) > 1 else None
        h = _ple(h, ple_norm[i], copies[0], p_rows, w_ple_proj, i, tm=512)
    return h.reshape(batch, seq, d_model)
```

```python
import functools
import math

import jax
import jax.numpy as jnp
from jax import lax
from jax.experimental import pallas as pl
from jax.experimental.pallas import tpu as pltpu

F32 = jnp.float32
BF16 = jnp.bfloat16

RMS_EPS = 1e-6
ROPE_THETA = 10000.0
LOG2_E = math.log2(math.e)

LANES = 128
BF16_SUBLANES = 16
VMEM_LIMIT_BYTES = 56 * 1024 * 1024

MLA_HEADS = 16
Q_LORA = 512
KV_LORA = 512
QK_NOPE = 128
QK_ROPE = 64
QK_HEAD = QK_NOPE + QK_ROPE
V_HEAD = 128
HEAD_PAD = 256

HGRN_HEADS = 16
HGRN_DK = 128
HGRN_DV = 128
CHUNK = 64
SUB = 16
HALF = SUB // 2

CONV_W = 3
F32_SUBLANES = 8
CONV_HALO = F32_SUBLANES


def _params(*sem):
    return pltpu.CompilerParams(dimension_semantics=sem, vmem_limit_bytes=VMEM_LIMIT_BYTES)


def _rms(x, g):
    return x * lax.rsqrt(jnp.mean(x * x, axis=-1, keepdims=True) + RMS_EPS) * g


def _sigmoid(x):
    return 1.0 / (1.0 + jnp.exp(-x))


def _dot(a, b):
    return jnp.dot(a, b, preferred_element_type=F32)


def _dot_nt(a, b):
    return lax.dot_general(a, b, (((1,), (1,)), ((), ())), preferred_element_type=F32)


def _dot_tn(a, b):
    return lax.dot_general(a, b, (((0,), (0,)), ((), ())), preferred_element_type=F32)


def _wspec(k, tn, layer, col, resident=False):
    mode = dict(pipeline_mode=pl.Buffered(1)) if resident else {}
    if layer is None:
        return pl.BlockSpec((k, tn), lambda *g: (0, col(*g)), **mode)
    return pl.BlockSpec((None, k, tn), lambda *g: (layer, 0, col(*g)), **mode)


def _pallas(body, ins, *, grid, in_specs, out_specs, out_shape, sem, name, scratch_shapes=(), casts=()):
    single = not isinstance(out_shape, (list, tuple))
    out_specs = [out_specs] if single else list(out_specs)
    out_shape = [out_shape] if single else list(out_shape)
    n_in, n_out, n_cast = len(ins), len(out_shape), len(casts)
    steps = math.prod(grid)

    def step_of(*g):
        idx = 0
        for size, pos in zip(grid, g):
            idx = idx * size + pos
        return idx

    c_in, c_in_specs, c_out_specs, c_out_shape = [], [], [], []
    for stack, layer in casts:
        _, rows, cols = stack.shape
        rb = rows // steps
        assert rows % steps == 0 and rb % BF16_SUBLANES == 0
        c_in.append(stack)
        c_in_specs.append(pl.BlockSpec((None, rb, cols), lambda *g, layer=layer: (layer, step_of(*g), 0)))
        c_out_specs.append(pl.BlockSpec((rb, cols), lambda *g: (step_of(*g), 0)))
        c_out_shape.append(jax.ShapeDtypeStruct((rows, cols), BF16))

    def kernel(*refs):
        outs_at = n_in + n_cast
        copies_at = outs_at + n_out
        for src, dst in zip(refs[n_in:outs_at], refs[copies_at:copies_at + n_cast]):
            dst[...] = src[...].astype(BF16)
        body(*refs[:n_in], *refs[outs_at:copies_at], *refs[copies_at + n_cast:])

    res = pl.pallas_call(
        kernel,
        grid=grid,
        in_specs=[*in_specs, *c_in_specs],
        out_specs=[*out_specs, *c_out_specs],
        out_shape=[*out_shape, *c_out_shape],
        scratch_shapes=list(scratch_shapes),
        compiler_params=_params(*sem),
        name=name,
    )(*ins, *c_in)
    outs = res[0] if single else list(res[:n_out])
    return outs, list(res[n_out:])


def _matmul_kernel(*refs, norm, residual):
    it = iter(refs)
    x_ref = next(it)
    g_ref = next(it) if norm else None
    w_ref = next(it)
    r_ref = next(it) if residual else None
    o_ref = next(it)
    xn_ref = next(it) if norm else None

    if norm:
        @pl.when(pl.program_id(1) == 0)
        def _():
            xn_ref[...] = _rms(x_ref[...], g_ref[...]).astype(BF16)
        acc = _dot(xn_ref[...], w_ref[...])
    else:
        acc = _dot(x_ref[...], w_ref[...])
    if residual:
        acc = r_ref[...] + acc
    o_ref[...] = acc.astype(o_ref.dtype)


def _matmul(x, w, layer, *, gain=None, residual=None, out_dtype, tm, tn, name, casts=()):
    m, k = x.shape
    n = w.shape[-1]
    tm, tn = min(tm, m), min(tn, n)
    assert m % tm == 0 and n % tn == 0
    norm = gain is not None
    ins = [x]
    specs = [pl.BlockSpec((tm, k), lambda i, j: (i, 0))]
    if norm:
        ins.append(gain.reshape(1, k))
        specs.append(pl.BlockSpec((1, k), lambda i, j: (0, 0)))
    ins.append(w)
    specs.append(_wspec(k, tn, layer, lambda i, j: j))
    if residual is not None:
        ins.append(residual)
        specs.append(pl.BlockSpec((tm, tn), lambda i, j: (i, j)))
    return _pallas(
        functools.partial(_matmul_kernel, norm=norm, residual=residual is not None), ins,
        grid=(m // tm, n // tn),
        in_specs=specs,
        out_specs=pl.BlockSpec((tm, tn), lambda i, j: (i, j)),
        out_shape=jax.ShapeDtypeStruct((m, n), out_dtype),
        scratch_shapes=[pltpu.VMEM((tm, k), BF16)] if norm else [],
        sem=("parallel", "arbitrary"),
        name=name,
        casts=casts,
    )


def _ffn_in_kernel(x_ref, g_ref, wg_ref, wu_ref, cw_ref, cb_ref, o_ref, xn_ref, gate_ref, carry_ref,
                   *, tm, tiles_per_seq):
    i = pl.program_id(0)
    j = pl.program_id(1)

    @pl.when(j == 0)
    def _():
        xn_ref[...] = _rms(x_ref[...], g_ref[...]).astype(BF16)

    seq_start = i % tiles_per_seq == 0

    @pl.when(seq_start)
    def _():
        gate_ref[:CONV_HALO, :] = jnp.zeros((CONV_HALO, gate_ref.shape[1]), F32)

    @pl.when(jnp.logical_not(seq_start))
    def _():
        gate_ref[:CONV_HALO, :] = carry_ref[j]

    gate_ref[CONV_HALO:, :] = _dot(xn_ref[...], wg_ref[...])
    up = _dot(xn_ref[...], wu_ref[...])
    carry_ref[j] = gate_ref[tm:, :]
    cw = cw_ref[...]
    conv = cb_ref[...] + gate_ref[pl.ds(CONV_HALO - 2, tm), :] * cw[0:1, :]
    conv = conv + gate_ref[pl.ds(CONV_HALO - 1, tm), :] * cw[1:2, :]
    conv = conv + gate_ref[pl.ds(CONV_HALO, tm), :] * cw[2:3, :]
    o_ref[...] = (conv * _sigmoid(conv) * up).astype(o_ref.dtype)


def _ffn_in(h, gain, w_in, layer, conv_w, conv_b, *, seq, tm, tn, casts=()):
    m, k = h.shape
    d_ff = w_in.shape[-1] // 2
    tm = min(tm, seq)
    assert seq % tm == 0 and d_ff % tn == 0 and tm % CONV_HALO == 0
    nj = d_ff // tn
    return _pallas(
        functools.partial(_ffn_in_kernel, tm=tm, tiles_per_seq=seq // tm),
        (h, gain.reshape(1, k), w_in, w_in, conv_w, conv_b.reshape(1, d_ff)),
        grid=(m // tm, nj),
        in_specs=[
            pl.BlockSpec((tm, k), lambda i, j: (i, 0)),
            pl.BlockSpec((1, k), lambda i, j: (0, 0)),
            _wspec(k, tn, layer, lambda i, j: j),
            _wspec(k, tn, layer, lambda i, j: j + nj),
            pl.BlockSpec((CONV_W, tn), lambda i, j: (0, j)),
            pl.BlockSpec((1, tn), lambda i, j: (0, j)),
        ],
        out_specs=pl.BlockSpec((tm, tn), lambda i, j: (i, j)),
        out_shape=jax.ShapeDtypeStruct((m, d_ff), BF16),
        scratch_shapes=[pltpu.VMEM((tm, k), BF16), pltpu.VMEM((tm + CONV_HALO, tn), F32),
                        pltpu.VMEM((nj, CONV_HALO, tn), F32)],
        sem=("arbitrary", "arbitrary"),
        name="ffn_in",
        casts=casts,
    )


def _ple_kernel(h_ref, g_ref, wg_ref, p_ref, wp_ref, o_ref):
    h = h_ref[...]
    proj = _dot(p_ref[...].astype(BF16), wp_ref[...])
    gate = _dot(_rms(h, g_ref[...]).astype(BF16), wg_ref[...])
    o_ref[...] = h + proj * _sigmoid(gate)


def _ple(h, gain, w_gate, p, w_proj, layer, *, tm):
    m, k = h.shape
    n = w_gate.shape[-1]
    kp = p.shape[2]
    tm = min(tm, m)
    assert m % tm == 0 and n == k
    gate_layer = layer if w_gate.ndim == 3 else None
    return pl.pallas_call(
        _ple_kernel,
        grid=(m // tm,),
        in_specs=[
            pl.BlockSpec((tm, k), lambda i: (i, 0)),
            pl.BlockSpec((1, k), lambda i: (0, 0)),
            _wspec(k, n, gate_layer, lambda i: 0),
            pl.BlockSpec((None, tm, kp), lambda i: (layer, i, 0)),
            _wspec(kp, n, layer, lambda i: 0),
        ],
        out_specs=pl.BlockSpec((tm, n), lambda i: (i, 0)),
        out_shape=jax.ShapeDtypeStruct((m, n), F32),
        compiler_params=_params("parallel"),
        name="ple",
    )(h, gain.reshape(1, k), w_gate, p, w_proj)


def _rope_table_kernel(pos_ref, inv_ref, c_ref, s_ref):
    ang = pos_ref[...].astype(F32) * inv_ref[...]
    lane = lax.broadcasted_iota(jnp.int32, ang.shape, 1)
    c = jnp.cos(ang)
    s = jnp.sin(ang)
    half = QK_ROPE // 2
    c_ref[...] = jnp.where(lane < QK_ROPE, c, 0.0)
    s_ref[...] = jnp.where(lane < half, -s, jnp.where(lane < QK_ROPE, s, 0.0))


def _rope_tables(positions, *, tm):
    m = positions.size
    tm = min(tm, m)
    half = QK_ROPE // 2
    inv_freq = ROPE_THETA ** (-jnp.arange(0, QK_ROPE, 2, dtype=F32) / QK_ROPE)
    inv_row = jnp.concatenate([inv_freq, inv_freq, jnp.zeros((LANES - 2 * half,), F32)]).reshape(1, LANES)
    out = jax.ShapeDtypeStruct((m, LANES), F32)
    spec = pl.BlockSpec((tm, LANES), lambda i: (i, 0))
    return pl.pallas_call(
        _rope_table_kernel,
        grid=(m // tm,),
        in_specs=[pl.BlockSpec((tm, 1), lambda i: (i, 0)), pl.BlockSpec((1, LANES), lambda i: (0, 0))],
        out_specs=[spec, spec],
        out_shape=[out, out],
        compiler_params=_params("parallel"),
        name="rope_tables",
    )(positions.reshape(m, 1), inv_row)


def _mla_proj_kernel(h_ref, g_ref, wa_ref, qn_ref, kvn_ref, wuq_ref, wukv_ref, gq_ref, gk_ref,
                     c_ref, s_ref, q_out, k_out, v_out):
    xn = _rms(h_ref[...], g_ref[...]).astype(BF16)
    c = _dot(xn, wa_ref[...])
    cq = _rms(c[:, :Q_LORA], qn_ref[...]).astype(BF16)
    ckv = _rms(c[:, Q_LORA:Q_LORA + KV_LORA], kvn_ref[...]).astype(BF16)
    k_pe = c[:, Q_LORA + KV_LORA:]
    cos_t, sin_t = c_ref[...], s_ref[...]

    def rope(r):
        return r * cos_t + pltpu.roll(r, LANES // 2, 1) * sin_t

    def row_total(sq):
        return jnp.sum(sq, axis=-1, keepdims=True)

    gq = gq_ref[...]
    gk = gk_ref[...]
    k_pe_sq = row_total(0.5 * (k_pe * k_pe))
    k_rope = rope(k_pe * gk[:, QK_NOPE:])

    for hd in range(MLA_HEADS):
        lo = hd * HEAD_PAD
        mid = lo + QK_NOPE
        hi = lo + HEAD_PAD
        qh = _dot(cq, wuq_ref[:, lo:hi])
        q_nope, q_rope = qh[:, :QK_NOPE], qh[:, QK_NOPE:]
        q_sq = row_total(q_nope * q_nope + 0.5 * (q_rope * q_rope))
        inv = lax.rsqrt(q_sq * (1.0 / QK_HEAD) + RMS_EPS)
        q_out[:, lo:mid] = (q_nope * inv * gq[:, :QK_NOPE]).astype(BF16)
        q_out[:, mid:hi] = rope(q_rope * inv * gq[:, QK_NOPE:]).astype(BF16)

        kvh = _dot(ckv, wukv_ref[:, lo:hi])
        kn = kvh[:, :QK_NOPE]
        inv_k = lax.rsqrt((row_total(kn * kn) + k_pe_sq) * (1.0 / QK_HEAD) + RMS_EPS)
        k_out[:, lo:mid] = (kn * inv_k * gk[:, :QK_NOPE]).astype(BF16)
        k_out[:, mid:hi] = (k_rope * inv_k).astype(BF16)
        v_out[:, hd * V_HEAD:(hd + 1) * V_HEAD] = kvh[:, QK_NOPE:].astype(BF16)


def _mla_proj(h, gain, w_a, q_norm, w_uq_pad, kv_norm, w_ukv, layer, gq_pad, gk_pad, tables, *, tm):
    m, k = h.shape
    tm = min(tm, m)
    assert m % tm == 0
    a_dim = w_a.shape[2]
    nq = MLA_HEADS * HEAD_PAD
    const = lambda i: (0, 0)
    row = lambda i: (i, 0)
    return pl.pallas_call(
        _mla_proj_kernel,
        grid=(m // tm,),
        in_specs=[
            pl.BlockSpec((tm, k), row),
            pl.BlockSpec((1, k), const),
            _wspec(k, a_dim, layer, lambda i: 0, resident=True),
            pl.BlockSpec((1, Q_LORA), const),
            pl.BlockSpec((1, KV_LORA), const),
            _wspec(Q_LORA, nq, layer, lambda i: 0, resident=True),
            _wspec(KV_LORA, nq, layer, lambda i: 0, resident=True),
            pl.BlockSpec((1, HEAD_PAD), const),
            pl.BlockSpec((1, HEAD_PAD), const),
            pl.BlockSpec((tm, LANES), row),
            pl.BlockSpec((tm, LANES), row),
        ],
        out_specs=[pl.BlockSpec((tm, nq), row), pl.BlockSpec((tm, nq), row),
                   pl.BlockSpec((tm, MLA_HEADS * V_HEAD), row)],
        out_shape=[jax.ShapeDtypeStruct((m, nq), BF16), jax.ShapeDtypeStruct((m, nq), BF16),
                   jax.ShapeDtypeStruct((m, MLA_HEADS * V_HEAD), BF16)],
        compiler_params=_params("parallel"),
        name="mla_proj",
    )(h, gain.reshape(1, k), w_a, q_norm.reshape(1, Q_LORA), kv_norm.reshape(1, KV_LORA),
      w_uq_pad, w_ukv, gq_pad, gk_pad, *tables)


def _attn_kernel(q_ref, k_ref, v_ref, o_ref, *, blk):
    seq = q_ref.shape[0]
    row = lax.broadcasted_iota(jnp.int32, (blk, blk), 0)
    col = lax.broadcasted_iota(jnp.int32, (blk, blk), 1)
    causal = col <= row
    for qi in range(seq // blk):
        q = q_ref[qi * blk:(qi + 1) * blk, :]
        m_run = jnp.full((blk, 1), -jnp.inf, F32)
        l_run = jnp.zeros((blk, 1), F32)
        acc = jnp.zeros((blk, V_HEAD), F32)
        for ki in range(qi + 1):
            s = _dot_nt(q, k_ref[ki * blk:(ki + 1) * blk, :])
            if ki == qi:
                s = jnp.where(causal, s, -jnp.inf)
            m_new = jnp.maximum(m_run, jnp.max(s, axis=-1, keepdims=True))
            p = jnp.exp2(s - m_new)
            alpha = jnp.exp2(m_run - m_new)
            l_run = alpha * l_run + jnp.sum(p, axis=-1, keepdims=True)
            acc = alpha * acc + _dot(p.astype(BF16), v_ref[ki * blk:(ki + 1) * blk, :])
            m_run = m_new
        o_ref[qi * blk:(qi + 1) * blk, :] = (acc / l_run).astype(o_ref.dtype)


def _attention(q, k, v, *, batch, seq, blk, casts=()):
    m = q.shape[0]
    blk = min(blk, seq)
    assert seq % blk == 0
    return _pallas(
        functools.partial(_attn_kernel, blk=blk), (q, k, v),
        grid=(batch, MLA_HEADS),
        in_specs=[
            pl.BlockSpec((seq, HEAD_PAD), lambda b, h: (b, h)),
            pl.BlockSpec((seq, HEAD_PAD), lambda b, h: (b, h)),
            pl.BlockSpec((seq, V_HEAD), lambda b, h: (b, h)),
        ],
        out_specs=pl.BlockSpec((seq, V_HEAD), lambda b, h: (b, h)),
        out_shape=jax.ShapeDtypeStruct((m, MLA_HEADS * V_HEAD), BF16),
        sem=("parallel", "parallel"),
        name="mla_attention",
        casts=casts,
    )


def _hgrn_prep(q_ref, f_ref, lb, tri, qs_ref, cs_ref, bs_ref, qi_ref, kd_ref, eb_ref, *, chunks):
    width = lb.shape[1]
    for c in range(chunks):
        rows = slice(c * CHUNK, (c + 1) * CHUNK)
        ff = f_ref[rows, :]
        e_abs = jnp.exp(-jnp.abs(ff))
        r_abs = 1.0 / (1.0 + e_abs)
        f = lb + (1.0 - lb) * jnp.where(ff >= 0.0, r_abs, e_abs * r_abs)
        log_f = jnp.log2(f)
        log_k = jnp.log2(1.0 - lb) - (jnp.maximum(ff, 0.0) * LOG2_E + jnp.log2(1.0 + e_abs))
        q_raw = q_ref[rows, :]
        q = q_raw * _sigmoid(q_raw)
        hi = log_f.astype(BF16)
        r1 = log_f - hi.astype(F32)
        mid = r1.astype(BF16)
        lo = (r1 - mid.astype(F32)).astype(BF16)
        parts = _dot(tri, jnp.concatenate([hi, mid, lo], axis=1))
        b = parts[:, :width] + parts[:, width:2 * width] + parts[:, 2 * width:]
        b_last = b[CHUNK - 1:CHUNK, :]
        c_dec = b - log_k
        qs_ref[rows, :] = q
        cs_ref[rows, :] = c_dec
        bs_ref[rows, :] = b
        qi_ref[rows, :] = (q * jnp.exp2(b)).astype(BF16)
        kd_ref[rows, :] = jnp.exp2(b_last - c_dec).astype(BF16)
        eb_ref[c] = jnp.exp2(b_last)


def _hgrn_intra(q, c_dec, b):
    lane = lax.broadcasted_iota(jnp.int32, (HALF, CHUNK), 1)
    srow = lax.broadcasted_iota(jnp.int32, (CHUNK, HGRN_DK), 0)
    a_rows = []
    for blk in range(CHUNK // SUB):
        r0 = blk * SUB
        halves = []
        for hf in range(2):
            t0 = r0 + hf * HALF
            q_t = q[t0:t0 + HALF, :]
            b_t = b[t0:t0 + HALF, :]
            a_half = jnp.zeros((HALF, CHUNK), F32)
            for sl in range((hf + 1) * HALF):
                s = r0 + sl
                col = jnp.sum(q_t * jnp.exp2(b_t - c_dec[s:s + 1, :]), axis=-1, keepdims=True)
                a_half = jnp.where(lane == s, col, a_half)
            halves.append(a_half)
        a_blk = jnp.concatenate(halves, axis=0)
        if blk > 0:
            ref_b = b[r0 - 1:r0, :]
            q_f = (q[r0:r0 + SUB, :] * jnp.exp2(b[r0:r0 + SUB, :] - ref_b)).astype(BF16)
            k_f = jnp.where(srow < r0, jnp.exp2(ref_b - c_dec), 0.0).astype(BF16)
            a_blk = a_blk + _dot_nt(q_f, k_f)
        a_rows.append(a_blk)
    t_idx = lax.broadcasted_iota(jnp.int32, (CHUNK, CHUNK), 0)
    s_idx = lax.broadcasted_iota(jnp.int32, (CHUNK, CHUNK), 1)
    return jnp.where(s_idx <= t_idx, jnp.concatenate(a_rows, axis=0), 0.0)


def _hgrn_kernel(q_ref, f_ref, v_ref, g_ref, lb_ref, on_ref, o_ref,
                 st_ref, qs_ref, cs_ref, bs_ref, qi_ref, kd_ref, eb_ref, a_ref, *, heads, chunks):
    @pl.when(pl.program_id(2) == 0)
    def _():
        st_ref[...] = jnp.zeros_like(st_ref)

    r = lax.broadcasted_iota(jnp.int32, (CHUNK, CHUNK), 0)
    c = lax.broadcasted_iota(jnp.int32, (CHUNK, CHUNK), 1)
    tri = (c <= r).astype(BF16)
    _hgrn_prep(q_ref, f_ref, lb_ref[...], tri, qs_ref, cs_ref, bs_ref, qi_ref, kd_ref, eb_ref, chunks=chunks)

    def build_a(ci, slot):
        rows = pl.ds(pl.multiple_of(ci * CHUNK, CHUNK), CHUNK)
        for hd in range(heads):
            cols = slice(hd * HGRN_DK, (hd + 1) * HGRN_DK)
            a_ref[slot, hd] = _hgrn_intra(qs_ref[rows, cols], cs_ref[rows, cols], bs_ref[rows, cols]).astype(BF16)

    def finish(ci, slot):
        rows = pl.ds(pl.multiple_of(ci * CHUNK, CHUNK), CHUNK)
        for hd in range(heads):
            cols = slice(hd * HGRN_DK, (hd + 1) * HGRN_DK)
            v16 = v_ref[rows, cols].astype(BF16)
            st = st_ref[hd]
            o = _dot_nt(qi_ref[rows, cols], st.astype(BF16)) + _dot(a_ref[slot, hd], v16)
            st_ref[hd] = st * eb_ref[ci, :, cols] + _dot_tn(v16, kd_ref[rows, cols])
            o = o * lax.rsqrt(jnp.mean(o * o, axis=-1, keepdims=True) + RMS_EPS) * on_ref[:, cols]
            gate = g_ref[rows, cols]
            o_ref[rows, cols] = (o * (gate * _sigmoid(gate))).astype(o_ref.dtype)

    build_a(0, 0)

    def body(ci, carry):
        slot = ci % 2
        finish(ci, slot)
        build_a(jnp.minimum(ci + 1, chunks - 1), 1 - slot)
        return carry

    lax.fori_loop(0, chunks, body, 0)


def _hgrn_recurrence(z, lb, o_gain, *, batch, seq, heads, tb, casts=()):
    m = z.shape[0]
    fdim = HGRN_HEADS * HGRN_DK
    tb = min(tb, seq)
    assert seq % tb == 0 and tb % CHUNK == 0 and HGRN_HEADS % heads == 0
    width = heads * HGRN_DK
    groups = HGRN_HEADS // heads
    nt = seq // tb
    chunks = tb // CHUNK

    def zspec(part):
        return pl.BlockSpec((tb, width), lambda b, g, t: (b * nt + t, part * groups + g))

    return _pallas(
        functools.partial(_hgrn_kernel, heads=heads, chunks=chunks),
        (z, z, z, z, lb.reshape(1, fdim), o_gain.reshape(1, fdim)),
        grid=(batch, groups, nt),
        in_specs=[zspec(0), zspec(1), zspec(2), zspec(3),
                  pl.BlockSpec((1, width), lambda b, g, t: (0, g)),
                  pl.BlockSpec((1, width), lambda b, g, t: (0, g))],
        out_specs=pl.BlockSpec((tb, width), lambda b, g, t: (b * nt + t, g)),
        out_shape=jax.ShapeDtypeStruct((m, fdim), BF16),
        scratch_shapes=[
            pltpu.VMEM((heads, HGRN_DV, HGRN_DK), F32),
            pltpu.VMEM((tb, width), F32),
            pltpu.VMEM((tb, width), F32),
            pltpu.VMEM((tb, width), F32),
            pltpu.VMEM((tb, width), BF16),
            pltpu.VMEM((tb, width), BF16),
            pltpu.VMEM((chunks, 1, width), F32),
            pltpu.VMEM((2, heads, CHUNK, CHUNK), BF16),
        ],
        sem=("parallel", "parallel", "arbitrary"),
        name="hgrn_recurrence",
        casts=casts,
    )


def _with_swapped_rope(t):
    half = QK_ROPE // 2
    return jnp.concatenate([t, t[..., -half:], t[..., -QK_ROPE:-half]], axis=-1)


def _mla_layer(h, tables, layer, mix_gain, w_a_pad, q_norm, w_uq_pad, kv_norm, w_ukv, gain_q, gain_k, w_o,
               *, batch, seq, casts=()):
    gq_pad = _with_swapped_rope(gain_q * (LOG2_E / math.sqrt(QK_HEAD))).reshape(1, HEAD_PAD)
    gk_pad = _with_swapped_rope(gain_k).reshape(1, HEAD_PAD)
    q, k, v = _mla_proj(h, mix_gain, w_a_pad, q_norm, w_uq_pad, kv_norm, w_ukv, layer, gq_pad, gk_pad, tables, tm=512)
    o, copies = _attention(q, k, v, batch=batch, seq=seq, blk=256, casts=casts)
    h, _ = _matmul(o, w_o, layer, residual=h, out_dtype=F32, tm=512, tn=2048, name="mla_out")
    return h, copies


def _hgrn_layer(h, layer, mix_gain, lb, w_in, o_gain, w_o, *, batch, seq, casts=()):
    z, _ = _matmul(h, w_in, None, gain=mix_gain, out_dtype=F32, tm=1024, tn=1024, name="hgrn_in")
    o, copies = _hgrn_recurrence(z, lb, o_gain, batch=batch, seq=seq, heads=4, tb=1024, casts=casts)
    h, _ = _matmul(o, w_o, layer, residual=h, out_dtype=F32, tm=512, tn=2048, name="hgrn_out")
    return h, copies


def kernel(x, p, positions, mix_norm, ffn_norm, ple_norm, mla_w_a, mla_q_norm, mla_w_uq, mla_kv_norm, mla_w_ukv, mla_qk_gain_q, mla_qk_gain_k, mla_w_o, hgrn_lb_logits, hgrn_w_in, hgrn_o_norm, hgrn_w_o, ffn_w_in, ffn_conv_w, ffn_conv_b, ffn_w_down, ple_w_proj, ple_w_gate):
    batch, seq, d_model = x.shape
    depth = p.shape[0]
    m = batch * seq
    tables = _rope_tables(positions, tm=1024)
    sm = jax.nn.softmax(hgrn_lb_logits.astype(F32), axis=0)
    lower_bounds = jnp.cumsum(sm, axis=0) - sm[0]

    n_mla = mla_w_a.shape[0]
    w_a_pad = _with_swapped_rope(mla_w_a).astype(BF16)
    w_uq_pad = _with_swapped_rope(mla_w_uq.reshape(n_mla, Q_LORA, MLA_HEADS, QK_HEAD))
    w_uq_pad = w_uq_pad.reshape(n_mla, Q_LORA, MLA_HEADS * HEAD_PAD).astype(BF16)
    w_ukv = mla_w_ukv.astype(BF16)
    w_mla_o = mla_w_o.astype(BF16)
    w_hgrn_o = hgrn_w_o.astype(BF16)
    w_ple_proj = ple_w_proj.astype(BF16)
    p_rows = p.reshape(depth, m, -1)

    h = x.reshape(m, d_model)
    w_hgrn_in = None
    for i in range(depth):
        j = i // 2
        if i % 2 == 0:
            h, (w_ffn_in,) = _mla_layer(h, tables, j, mix_norm[i], w_a_pad, mla_q_norm[j], w_uq_pad, mla_kv_norm[j],
                                        w_ukv, mla_qk_gain_q[j], mla_qk_gain_k[j], w_mla_o, batch=batch, seq=seq,
                                        casts=[(ffn_w_in, i)])
        else:
            h, (w_ffn_in,) = _hgrn_layer(h, j, mix_norm[i], lower_bounds[i], w_hgrn_in, hgrn_o_norm[j], w_hgrn_o,
                                         batch=batch, seq=seq, casts=[(ffn_w_in, i)])
        act, (w_ffn_down,) = _ffn_in(h, ffn_norm[i], w_ffn_in, None, ffn_conv_w[i], ffn_conv_b[i],
                                     seq=seq, tm=1024, tn=512, casts=[(ffn_w_down, i)])
        down_casts = [(ple_w_gate, i)]
        if i + 1 < depth and (i + 1) % 2 == 1:
            down_casts.append((hgrn_w_in, (i + 1) // 2))
        h, copies = _matmul(act, w_ffn_down, None, residual=h, out_dtype=F32, tm=1024, tn=512, name="ffn_down",
                            casts=down_casts)
        w_hgrn_in = copies[1] if len(copies) > 1 else None
        h = _ple(h, ple_norm[i], copies[0], p_rows, w_ple_proj, i, tm=512)
    return h.reshape(batch, seq, d_model)
```

```python
import functools
import math

import jax
import jax.numpy as jnp
from jax import lax
from jax.experimental import pallas as pl
from jax.experimental.pallas import tpu as pltpu

F32 = jnp.float32
BF16 = jnp.bfloat16

RMS_EPS = 1e-6
ROPE_THETA = 10000.0
LOG2_E = math.log2(math.e)

LANES = 128
BF16_SUBLANES = 16
VMEM_LIMIT_BYTES = 56 * 1024 * 1024

MLA_HEADS = 16
Q_LORA = 512
KV_LORA = 512
QK_NOPE = 128
QK_ROPE = 64
QK_HEAD = QK_NOPE + QK_ROPE
V_HEAD = 128
HEAD_PAD = 256

HGRN_HEADS = 16
HGRN_DK = 128
HGRN_DV = 128
CHUNK = 64
SUB = 16
HALF = SUB // 2

CONV_W = 3
F32_SUBLANES = 8
CONV_HALO = F32_SUBLANES


def _params(*sem):
    return pltpu.CompilerParams(dimension_semantics=sem, vmem_limit_bytes=VMEM_LIMIT_BYTES)


def _rms(x, g):
    return x * lax.rsqrt(jnp.mean(x * x, axis=-1, keepdims=True) + RMS_EPS) * g


def _sigmoid(x):
    return 1.0 / (1.0 + jnp.exp(-x))


def _dot(a, b):
    return jnp.dot(a, b, preferred_element_type=F32)


def _dot_nt(a, b):
    return lax.dot_general(a, b, (((1,), (1,)), ((), ())), preferred_element_type=F32)


def _dot_tn(a, b):
    return lax.dot_general(a, b, (((0,), (0,)), ((), ())), preferred_element_type=F32)


def _wspec(k, tn, layer, col, resident=False):
    mode = dict(pipeline_mode=pl.Buffered(1)) if resident else {}
    if layer is None:
        return pl.BlockSpec((k, tn), lambda *g: (0, col(*g)), **mode)
    return pl.BlockSpec((None, k, tn), lambda *g: (layer, 0, col(*g)), **mode)


def _pallas(body, ins, *, grid, in_specs, out_specs, out_shape, sem, name, scratch_shapes=(), casts=()):
    single = not isinstance(out_shape, (list, tuple))
    out_specs = [out_specs] if single else list(out_specs)
    out_shape = [out_shape] if single else list(out_shape)
    n_in, n_out, n_cast = len(ins), len(out_shape), len(casts)
    steps = math.prod(grid)

    def step_of(*g):
        idx = 0
        for size, pos in zip(grid, g):
            idx = idx * size + pos
        return idx

    c_in, c_in_specs, c_out_specs, c_out_shape, c_shapes = [], [], [], [], []
    for stack, layer in casts:
        layers, rows, cols = stack.shape
        size = rows * cols
        view_cols = next(c for c in (cols, cols // 2, cols // 4, cols // 8)
                         if c % LANES == 0 and (size // c) % (steps * BF16_SUBLANES) == 0)
        view_rows = size // view_cols
        rb = view_rows // steps
        c_in.append(stack.reshape(layers, view_rows, view_cols))
        c_in_specs.append(pl.BlockSpec((None, rb, view_cols), lambda *g, layer=layer: (layer, step_of(*g), 0)))
        c_out_specs.append(pl.BlockSpec((rb, view_cols), lambda *g: (step_of(*g), 0)))
        c_out_shape.append(jax.ShapeDtypeStruct((view_rows, view_cols), BF16))
        c_shapes.append((rows, cols))

    def kernel(*refs):
        outs_at = n_in + n_cast
        copies_at = outs_at + n_out
        for src, dst in zip(refs[n_in:outs_at], refs[copies_at:copies_at + n_cast]):
            dst[...] = src[...].astype(BF16)
        body(*refs[:n_in], *refs[outs_at:copies_at], *refs[copies_at + n_cast:])

    res = pl.pallas_call(
        kernel,
        grid=grid,
        in_specs=[*in_specs, *c_in_specs],
        out_specs=[*out_specs, *c_out_specs],
        out_shape=[*out_shape, *c_out_shape],
        scratch_shapes=list(scratch_shapes),
        compiler_params=_params(*sem),
        name=name,
    )(*ins, *c_in)
    outs = res[0] if single else list(res[:n_out])
    return outs, [copy.reshape(shape) for copy, shape in zip(res[n_out:], c_shapes)]


def _matmul_kernel(*refs, norm, residual):
    it = iter(refs)
    x_ref = next(it)
    g_ref = next(it) if norm else None
    w_ref = next(it)
    r_ref = next(it) if residual else None
    o_ref = next(it)
    xn_ref = next(it) if norm else None

    if norm:
        @pl.when(pl.program_id(1) == 0)
        def _():
            xn_ref[...] = _rms(x_ref[...], g_ref[...]).astype(BF16)
        acc = _dot(xn_ref[...], w_ref[...])
    else:
        acc = _dot(x_ref[...], w_ref[...])
    if residual:
        acc = r_ref[...] + acc
    o_ref[...] = acc.astype(o_ref.dtype)


def _matmul(x, w, layer, *, gain=None, residual=None, out_dtype, tm, tn, name, casts=()):
    m, k = x.shape
    n = w.shape[-1]
    tm, tn = min(tm, m), min(tn, n)
    assert m % tm == 0 and n % tn == 0
    norm = gain is not None
    ins = [x]
    specs = [pl.BlockSpec((tm, k), lambda i, j: (i, 0))]
    if norm:
        ins.append(gain.reshape(1, k))
        specs.append(pl.BlockSpec((1, k), lambda i, j: (0, 0)))
    ins.append(w)
    specs.append(_wspec(k, tn, layer, lambda i, j: j, resident=tn == n))
    if residual is not None:
        ins.append(residual)
        specs.append(pl.BlockSpec((tm, tn), lambda i, j: (i, j)))
    return _pallas(
        functools.partial(_matmul_kernel, norm=norm, residual=residual is not None), ins,
        grid=(m // tm, n // tn),
        in_specs=specs,
        out_specs=pl.BlockSpec((tm, tn), lambda i, j: (i, j)),
        out_shape=jax.ShapeDtypeStruct((m, n), out_dtype),
        scratch_shapes=[pltpu.VMEM((tm, k), BF16)] if norm else [],
        sem=("parallel", "arbitrary"),
        name=name,
        casts=casts,
    )


def _ffn_in_kernel(x_ref, g_ref, wg_ref, wu_ref, cw_ref, cb_ref, o_ref, xn_ref, gate_ref, carry_ref,
                   *, tm, tiles_per_seq):
    i = pl.program_id(0)
    j = pl.program_id(1)

    @pl.when(j == 0)
    def _():
        xn_ref[...] = _rms(x_ref[...], g_ref[...]).astype(BF16)

    seq_start = i % tiles_per_seq == 0

    @pl.when(seq_start)
    def _():
        gate_ref[:CONV_HALO, :] = jnp.zeros((CONV_HALO, gate_ref.shape[1]), F32)

    @pl.when(jnp.logical_not(seq_start))
    def _():
        gate_ref[:CONV_HALO, :] = carry_ref[j]

    gate_ref[CONV_HALO:, :] = _dot(xn_ref[...], wg_ref[...])
    up = _dot(xn_ref[...], wu_ref[...])
    carry_ref[j] = gate_ref[tm:, :]
    cw = cw_ref[...]
    conv = cb_ref[...] + gate_ref[pl.ds(CONV_HALO - 2, tm), :] * cw[0:1, :]
    conv = conv + gate_ref[pl.ds(CONV_HALO - 1, tm), :] * cw[1:2, :]
    conv = conv + gate_ref[pl.ds(CONV_HALO, tm), :] * cw[2:3, :]
    o_ref[...] = (conv * _sigmoid(conv) * up).astype(o_ref.dtype)


def _ffn_in(h, gain, w_in, layer, conv_w, conv_b, *, seq, tm, tn, casts=()):
    m, k = h.shape
    d_ff = w_in.shape[-1] // 2
    tm = min(tm, seq)
    assert seq % tm == 0 and d_ff % tn == 0 and tm % CONV_HALO == 0
    nj = d_ff // tn
    return _pallas(
        functools.partial(_ffn_in_kernel, tm=tm, tiles_per_seq=seq // tm),
        (h, gain.reshape(1, k), w_in, w_in, conv_w, conv_b.reshape(1, d_ff)),
        grid=(m // tm, nj),
        in_specs=[
            pl.BlockSpec((tm, k), lambda i, j: (i, 0)),
            pl.BlockSpec((1, k), lambda i, j: (0, 0)),
            _wspec(k, tn, layer, lambda i, j: j),
            _wspec(k, tn, layer, lambda i, j: j + nj),
            pl.BlockSpec((CONV_W, tn), lambda i, j: (0, j)),
            pl.BlockSpec((1, tn), lambda i, j: (0, j)),
        ],
        out_specs=pl.BlockSpec((tm, tn), lambda i, j: (i, j)),
        out_shape=jax.ShapeDtypeStruct((m, d_ff), BF16),
        scratch_shapes=[pltpu.VMEM((tm, k), BF16), pltpu.VMEM((tm + CONV_HALO, tn), F32),
                        pltpu.VMEM((nj, CONV_HALO, tn), F32)],
        sem=("arbitrary", "arbitrary"),
        name="ffn_in",
        casts=casts,
    )


def _ple_kernel(h_ref, g_ref, wg_ref, p_ref, wp_ref, o_ref):
    h = h_ref[...]
    proj = _dot(p_ref[...].astype(BF16), wp_ref[...])
    gate = _dot(_rms(h, g_ref[...]).astype(BF16), wg_ref[...])
    o_ref[...] = h + proj * _sigmoid(gate)


def _ple(h, gain, w_gate, p, w_proj, layer, *, tm):
    m, k = h.shape
    n = w_gate.shape[-1]
    kp = p.shape[2]
    tm = min(tm, m)
    assert m % tm == 0 and n == k
    gate_layer = layer if w_gate.ndim == 3 else None
    return pl.pallas_call(
        _ple_kernel,
        grid=(m // tm,),
        in_specs=[
            pl.BlockSpec((tm, k), lambda i: (i, 0)),
            pl.BlockSpec((1, k), lambda i: (0, 0)),
            _wspec(k, n, gate_layer, lambda i: 0, resident=True),
            pl.BlockSpec((None, tm, kp), lambda i: (layer, i, 0)),
            _wspec(kp, n, layer, lambda i: 0, resident=True),
        ],
        out_specs=pl.BlockSpec((tm, n), lambda i: (i, 0)),
        out_shape=jax.ShapeDtypeStruct((m, n), F32),
        compiler_params=_params("parallel"),
        name="ple",
    )(h, gain.reshape(1, k), w_gate, p, w_proj)


def _rope_table_kernel(pos_ref, inv_ref, c_ref, s_ref):
    ang = pos_ref[...].astype(F32) * inv_ref[...]
    lane = lax.broadcasted_iota(jnp.int32, ang.shape, 1)
    c = jnp.cos(ang)
    s = jnp.sin(ang)
    half = QK_ROPE // 2
    c_ref[...] = jnp.where(lane < QK_ROPE, c, 0.0)
    s_ref[...] = jnp.where(lane < half, -s, jnp.where(lane < QK_ROPE, s, 0.0))


def _rope_tables(positions, *, tm):
    m = positions.size
    tm = min(tm, m)
    half = QK_ROPE // 2
    inv_freq = ROPE_THETA ** (-jnp.arange(0, QK_ROPE, 2, dtype=F32) / QK_ROPE)
    inv_row = jnp.concatenate([inv_freq, inv_freq, jnp.zeros((LANES - 2 * half,), F32)]).reshape(1, LANES)
    out = jax.ShapeDtypeStruct((m, LANES), F32)
    spec = pl.BlockSpec((tm, LANES), lambda i: (i, 0))
    return pl.pallas_call(
        _rope_table_kernel,
        grid=(m // tm,),
        in_specs=[pl.BlockSpec((tm, 1), lambda i: (i, 0)), pl.BlockSpec((1, LANES), lambda i: (0, 0))],
        out_specs=[spec, spec],
        out_shape=[out, out],
        compiler_params=_params("parallel"),
        name="rope_tables",
    )(positions.reshape(m, 1), inv_row)


def _mla_proj_kernel(h_ref, g_ref, wa_ref, qn_ref, kvn_ref, wuq_ref, wukv_ref, gq_ref, gk_ref,
                     c_ref, s_ref, q_out, k_out, v_out):
    xn = _rms(h_ref[...], g_ref[...]).astype(BF16)
    c = _dot(xn, wa_ref[...])
    cq = _rms(c[:, :Q_LORA], qn_ref[...]).astype(BF16)
    ckv = _rms(c[:, Q_LORA:Q_LORA + KV_LORA], kvn_ref[...]).astype(BF16)
    k_pe = c[:, Q_LORA + KV_LORA:]
    cos_t, sin_t = c_ref[...], s_ref[...]

    def rope(r):
        return r * cos_t + pltpu.roll(r, LANES // 2, 1) * sin_t

    def row_total(sq):
        return jnp.sum(sq, axis=-1, keepdims=True)

    gq = gq_ref[...]
    gk = gk_ref[...]
    k_pe_sq = row_total(0.5 * (k_pe * k_pe))
    k_rope = rope(k_pe * gk[:, QK_NOPE:])

    for hd in range(MLA_HEADS):
        lo = hd * HEAD_PAD
        mid = lo + QK_NOPE
        hi = lo + HEAD_PAD
        qh = _dot(cq, wuq_ref[:, lo:hi])
        q_nope, q_rope = qh[:, :QK_NOPE], qh[:, QK_NOPE:]
        q_sq = row_total(q_nope * q_nope + 0.5 * (q_rope * q_rope))
        inv = lax.rsqrt(q_sq * (1.0 / QK_HEAD) + RMS_EPS)
        q_out[:, lo:mid] = (q_nope * inv * gq[:, :QK_NOPE]).astype(BF16)
        q_out[:, mid:hi] = rope(q_rope * inv * gq[:, QK_NOPE:]).astype(BF16)

        kvh = _dot(ckv, wukv_ref[:, lo:hi])
        kn = kvh[:, :QK_NOPE]
        inv_k = lax.rsqrt((row_total(kn * kn) + k_pe_sq) * (1.0 / QK_HEAD) + RMS_EPS)
        k_out[:, lo:mid] = (kn * inv_k * gk[:, :QK_NOPE]).astype(BF16)
        k_out[:, mid:hi] = (k_rope * inv_k).astype(BF16)
        v_out[:, hd * V_HEAD:(hd + 1) * V_HEAD] = kvh[:, QK_NOPE:].astype(BF16)


def _mla_proj(h, gain, w_a, q_norm, w_uq_pad, kv_norm, w_ukv, layer, gq_pad, gk_pad, tables, *, tm, casts=()):
    m, k = h.shape
    tm = min(tm, m)
    assert m % tm == 0
    a_dim = w_a.shape[2]
    nq = MLA_HEADS * HEAD_PAD
    const = lambda i: (0, 0)
    row = lambda i: (i, 0)
    return _pallas(
        _mla_proj_kernel,
        (h, gain.reshape(1, k), w_a, q_norm.reshape(1, Q_LORA), kv_norm.reshape(1, KV_LORA),
         w_uq_pad, w_ukv, gq_pad, gk_pad, *tables),
        grid=(m // tm,),
        in_specs=[
            pl.BlockSpec((tm, k), row),
            pl.BlockSpec((1, k), const),
            _wspec(k, a_dim, layer, lambda i: 0, resident=True),
            pl.BlockSpec((1, Q_LORA), const),
            pl.BlockSpec((1, KV_LORA), const),
            _wspec(Q_LORA, nq, layer, lambda i: 0, resident=True),
            _wspec(KV_LORA, nq, layer, lambda i: 0, resident=True),
            pl.BlockSpec((1, HEAD_PAD), const),
            pl.BlockSpec((1, HEAD_PAD), const),
            pl.BlockSpec((tm, LANES), row),
            pl.BlockSpec((tm, LANES), row),
        ],
        out_specs=[pl.BlockSpec((tm, nq), row), pl.BlockSpec((tm, nq), row),
                   pl.BlockSpec((tm, MLA_HEADS * V_HEAD), row)],
        out_shape=[jax.ShapeDtypeStruct((m, nq), BF16), jax.ShapeDtypeStruct((m, nq), BF16),
                   jax.ShapeDtypeStruct((m, MLA_HEADS * V_HEAD), BF16)],
        sem=("parallel",),
        name="mla_proj",
        casts=casts,
    )


def _attn_kernel(q_ref, k_ref, v_ref, o_ref, *, blk):
    seq = q_ref.shape[0]
    row = lax.broadcasted_iota(jnp.int32, (blk, blk), 0)
    col = lax.broadcasted_iota(jnp.int32, (blk, blk), 1)
    causal = col <= row
    for qi in range(seq // blk):
        q = q_ref[qi * blk:(qi + 1) * blk, :]
        m_run = jnp.full((blk, 1), -jnp.inf, F32)
        l_run = jnp.zeros((blk, 1), F32)
        acc = jnp.zeros((blk, V_HEAD), F32)
        for ki in range(qi + 1):
            s = _dot_nt(q, k_ref[ki * blk:(ki + 1) * blk, :])
            if ki == qi:
                s = jnp.where(causal, s, -jnp.inf)
            m_new = jnp.maximum(m_run, jnp.max(s, axis=-1, keepdims=True))
            p = jnp.exp2(s - m_new)
            alpha = jnp.exp2(m_run - m_new)
            l_run = alpha * l_run + jnp.sum(p, axis=-1, keepdims=True)
            acc = alpha * acc + _dot(p.astype(BF16), v_ref[ki * blk:(ki + 1) * blk, :])
            m_run = m_new
        o_ref[qi * blk:(qi + 1) * blk, :] = (acc / l_run).astype(o_ref.dtype)


def _attention(q, k, v, *, batch, seq, blk, casts=()):
    m = q.shape[0]
    blk = min(blk, seq)
    assert seq % blk == 0
    return _pallas(
        functools.partial(_attn_kernel, blk=blk), (q, k, v),
        grid=(batch, MLA_HEADS),
        in_specs=[
            pl.BlockSpec((seq, HEAD_PAD), lambda b, h: (b, h)),
            pl.BlockSpec((seq, HEAD_PAD), lambda b, h: (b, h)),
            pl.BlockSpec((seq, V_HEAD), lambda b, h: (b, h)),
        ],
        out_specs=pl.BlockSpec((seq, V_HEAD), lambda b, h: (b, h)),
        out_shape=jax.ShapeDtypeStruct((m, MLA_HEADS * V_HEAD), BF16),
        sem=("parallel", "parallel"),
        name="mla_attention",
        casts=casts,
    )


def _hgrn_prep(q_ref, f_ref, lb, tri, qs_ref, cs_ref, bs_ref, qi_ref, kd_ref, eb_ref, *, chunks):
    width = lb.shape[1]
    for c in range(chunks):
        rows = slice(c * CHUNK, (c + 1) * CHUNK)
        ff = f_ref[rows, :]
        e_abs = jnp.exp(-jnp.abs(ff))
        r_abs = 1.0 / (1.0 + e_abs)
        f = lb + (1.0 - lb) * jnp.where(ff >= 0.0, r_abs, e_abs * r_abs)
        log_f = jnp.log2(f)
        log_k = jnp.log2(1.0 - lb) - (jnp.maximum(ff, 0.0) * LOG2_E + jnp.log2(1.0 + e_abs))
        q_raw = q_ref[rows, :]
        q = q_raw * _sigmoid(q_raw)
        hi = log_f.astype(BF16)
        r1 = log_f - hi.astype(F32)
        mid = r1.astype(BF16)
        lo = (r1 - mid.astype(F32)).astype(BF16)
        parts = _dot(tri, jnp.concatenate([hi, mid, lo], axis=1))
        b = parts[:, :width] + parts[:, width:2 * width] + parts[:, 2 * width:]
        b_last = b[CHUNK - 1:CHUNK, :]
        c_dec = b - log_k
        qs_ref[rows, :] = q
        cs_ref[rows, :] = c_dec
        bs_ref[rows, :] = b
        qi_ref[rows, :] = (q * jnp.exp2(b)).astype(BF16)
        kd_ref[rows, :] = jnp.exp2(b_last - c_dec).astype(BF16)
        eb_ref[c] = jnp.exp2(b_last)


def _hgrn_intra(q, c_dec, b):
    lane = lax.broadcasted_iota(jnp.int32, (HALF, CHUNK), 1)
    srow = lax.broadcasted_iota(jnp.int32, (CHUNK, HGRN_DK), 0)
    a_rows = []
    for blk in range(CHUNK // SUB):
        r0 = blk * SUB
        halves = []
        for hf in range(2):
            t0 = r0 + hf * HALF
            q_t = q[t0:t0 + HALF, :]
            b_t = b[t0:t0 + HALF, :]
            a_half = jnp.zeros((HALF, CHUNK), F32)
            for sl in range((hf + 1) * HALF):
                s = r0 + sl
                col = jnp.sum(q_t * jnp.exp2(b_t - c_dec[s:s + 1, :]), axis=-1, keepdims=True)
                a_half = jnp.where(lane == s, col, a_half)
            halves.append(a_half)
        a_blk = jnp.concatenate(halves, axis=0)
        if blk > 0:
            ref_b = b[r0 - 1:r0, :]
            q_f = (q[r0:r0 + SUB, :] * jnp.exp2(b[r0:r0 + SUB, :] - ref_b)).astype(BF16)
            k_f = jnp.where(srow < r0, jnp.exp2(ref_b - c_dec), 0.0).astype(BF16)
            a_blk = a_blk + _dot_nt(q_f, k_f)
        a_rows.append(a_blk)
    t_idx = lax.broadcasted_iota(jnp.int32, (CHUNK, CHUNK), 0)
    s_idx = lax.broadcasted_iota(jnp.int32, (CHUNK, CHUNK), 1)
    return jnp.where(s_idx <= t_idx, jnp.concatenate(a_rows, axis=0), 0.0)


def _hgrn_kernel(q_ref, f_ref, v_ref, g_ref, lb_ref, on_ref, o_ref,
                 st_ref, qs_ref, cs_ref, bs_ref, qi_ref, kd_ref, eb_ref, a_ref, *, heads, chunks):
    @pl.when(pl.program_id(2) == 0)
    def _():
        st_ref[...] = jnp.zeros_like(st_ref)

    r = lax.broadcasted_iota(jnp.int32, (CHUNK, CHUNK), 0)
    c = lax.broadcasted_iota(jnp.int32, (CHUNK, CHUNK), 1)
    tri = (c <= r).astype(BF16)
    _hgrn_prep(q_ref, f_ref, lb_ref[...], tri, qs_ref, cs_ref, bs_ref, qi_ref, kd_ref, eb_ref, chunks=chunks)

    def build_a(ci, slot):
        rows = pl.ds(pl.multiple_of(ci * CHUNK, CHUNK), CHUNK)
        for hd in range(heads):
            cols = slice(hd * HGRN_DK, (hd + 1) * HGRN_DK)
            a_ref[slot, hd] = _hgrn_intra(qs_ref[rows, cols], cs_ref[rows, cols], bs_ref[rows, cols]).astype(BF16)

    def finish(ci, slot):
        rows = pl.ds(pl.multiple_of(ci * CHUNK, CHUNK), CHUNK)
        for hd in range(heads):
            cols = slice(hd * HGRN_DK, (hd + 1) * HGRN_DK)
            v16 = v_ref[rows, cols].astype(BF16)
            st = st_ref[hd]
            o = _dot_nt(qi_ref[rows, cols], st.astype(BF16)) + _dot(a_ref[slot, hd], v16)
            st_ref[hd] = st * eb_ref[ci, :, cols] + _dot_tn(v16, kd_ref[rows, cols])
            o = o * lax.rsqrt(jnp.mean(o * o, axis=-1, keepdims=True) + RMS_EPS) * on_ref[:, cols]
            gate = g_ref[rows, cols]
            o_ref[rows, cols] = (o * (gate * _sigmoid(gate))).astype(o_ref.dtype)

    build_a(0, 0)

    def body(ci, carry):
        slot = ci % 2
        finish(ci, slot)
        build_a(jnp.minimum(ci + 1, chunks - 1), 1 - slot)
        return carry

    lax.fori_loop(0, chunks, body, 0)


def _hgrn_recurrence(z, lb, o_gain, *, batch, seq, heads, tb, casts=()):
    m = z.shape[0]
    fdim = HGRN_HEADS * HGRN_DK
    tb = min(tb, seq)
    assert seq % tb == 0 and tb % CHUNK == 0 and HGRN_HEADS % heads == 0
    width = heads * HGRN_DK
    groups = HGRN_HEADS // heads
    nt = seq // tb
    chunks = tb // CHUNK

    def zspec(part):
        return pl.BlockSpec((tb, width), lambda b, g, t: (b * nt + t, part * groups + g))

    return _pallas(
        functools.partial(_hgrn_kernel, heads=heads, chunks=chunks),
        (z, z, z, z, lb.reshape(1, fdim), o_gain.reshape(1, fdim)),
        grid=(batch, groups, nt),
        in_specs=[zspec(0), zspec(1), zspec(2), zspec(3),
                  pl.BlockSpec((1, width), lambda b, g, t: (0, g)),
                  pl.BlockSpec((1, width), lambda b, g, t: (0, g))],
        out_specs=pl.BlockSpec((tb, width), lambda b, g, t: (b * nt + t, g)),
        out_shape=jax.ShapeDtypeStruct((m, fdim), BF16),
        scratch_shapes=[
            pltpu.VMEM((heads, HGRN_DV, HGRN_DK), F32),
            pltpu.VMEM((tb, width), F32),
            pltpu.VMEM((tb, width), F32),
            pltpu.VMEM((tb, width), F32),
            pltpu.VMEM((tb, width), BF16),
            pltpu.VMEM((tb, width), BF16),
            pltpu.VMEM((chunks, 1, width), F32),
            pltpu.VMEM((2, heads, CHUNK, CHUNK), BF16),
        ],
        sem=("parallel", "parallel", "arbitrary"),
        name="hgrn_recurrence",
        casts=casts,
    )


def _with_swapped_rope(t):
    half = QK_ROPE // 2
    return jnp.concatenate([t, t[..., -half:], t[..., -QK_ROPE:-half]], axis=-1)


def _mla_layer(h, tables, layer, mix_gain, w_a_pad, q_norm, w_uq_pad, kv_norm, w_ukv, gain_q, gain_k, w_o,
               *, batch, seq, casts=()):
    gq_pad = _with_swapped_rope(gain_q * (LOG2_E / math.sqrt(QK_HEAD))).reshape(1, HEAD_PAD)
    gk_pad = _with_swapped_rope(gain_k).reshape(1, HEAD_PAD)
    (q, k, v), (w_o16,) = _mla_proj(h, mix_gain, w_a_pad, q_norm, w_uq_pad, kv_norm, w_ukv, layer, gq_pad, gk_pad,
                                    tables, tm=512, casts=[(w_o, layer)])
    o, copies = _attention(q, k, v, batch=batch, seq=seq, blk=256, casts=casts)
    h, _ = _matmul(o, w_o16, None, residual=h, out_dtype=F32, tm=1024, tn=2048, name="mla_out")
    return h, copies


def _hgrn_layer(h, layer, mix_gain, lb, w_in, o_gain, w_o, *, batch, seq, casts=()):
    z, (w_o16,) = _matmul(h, w_in, None, gain=mix_gain, out_dtype=F32, tm=1024, tn=1024, name="hgrn_in",
                          casts=[(w_o, layer)])
    o, copies = _hgrn_recurrence(z, lb, o_gain, batch=batch, seq=seq, heads=4, tb=1024, casts=casts)
    h, _ = _matmul(o, w_o16, None, residual=h, out_dtype=F32, tm=1024, tn=2048, name="hgrn_out")
    return h, copies


def kernel(x, p, positions, mix_norm, ffn_norm, ple_norm, mla_w_a, mla_q_norm, mla_w_uq, mla_kv_norm, mla_w_ukv, mla_qk_gain_q, mla_qk_gain_k, mla_w_o, hgrn_lb_logits, hgrn_w_in, hgrn_o_norm, hgrn_w_o, ffn_w_in, ffn_conv_w, ffn_conv_b, ffn_w_down, ple_w_proj, ple_w_gate):
    batch, seq, d_model = x.shape
    depth = p.shape[0]
    m = batch * seq
    tables = _rope_tables(positions, tm=1024)
    sm = jax.nn.softmax(hgrn_lb_logits.astype(F32), axis=0)
    lower_bounds = jnp.cumsum(sm, axis=0) - sm[0]

    n_mla = mla_w_a.shape[0]
    w_a_pad = _with_swapped_rope(mla_w_a).astype(BF16)
    w_uq_pad = _with_swapped_rope(mla_w_uq.reshape(n_mla, Q_LORA, MLA_HEADS, QK_HEAD))
    w_uq_pad = w_uq_pad.reshape(n_mla, Q_LORA, MLA_HEADS * HEAD_PAD).astype(BF16)
    w_ukv = mla_w_ukv.astype(BF16)
    w_ple_proj = ple_w_proj.astype(BF16)
    p_rows = p.reshape(depth, m, -1)

    h = x.reshape(m, d_model)
    w_hgrn_in = None
    for i in range(depth):
        j = i // 2
        if i % 2 == 0:
            h, (w_ffn_in,) = _mla_layer(h, tables, j, mix_norm[i], w_a_pad, mla_q_norm[j], w_uq_pad, mla_kv_norm[j],
                                        w_ukv, mla_qk_gain_q[j], mla_qk_gain_k[j], mla_w_o, batch=batch, seq=seq,
                                        casts=[(ffn_w_in, i)])
        else:
            h, (w_ffn_in,) = _hgrn_layer(h, j, mix_norm[i], lower_bounds[i], w_hgrn_in, hgrn_o_norm[j], hgrn_w_o,
                                         batch=batch, seq=seq, casts=[(ffn_w_in, i)])
        act, (w_ffn_down,) = _ffn_in(h, ffn_norm[i], w_ffn_in, None, ffn_conv_w[i], ffn_conv_b[i],
                                     seq=seq, tm=1024, tn=512, casts=[(ffn_w_down, i)])
        down_casts = [(ple_w_gate, i)]
        if i + 1 < depth and (i + 1) % 2 == 1:
            down_casts.append((hgrn_w_in, (i + 1) // 2))
        h, copies = _matmul(act, w_ffn_down, None, residual=h, out_dtype=F32, tm=1024, tn=512, name="ffn_down",
                            casts=down_casts)
        w_hgrn_in = copies[1] if len(copies) > 1 else None
        h = _ple(h, ple_norm[i], copies[0], p_rows, w_ple_proj, i, tm=1024)
    return h.reshape(batch, seq, d_model)
```

```python
import functools
import math

import jax
import jax.numpy as jnp
from jax import lax
from jax.experimental import pallas as pl
from jax.experimental.pallas import tpu as pltpu

F32 = jnp.float32
BF16 = jnp.bfloat16

RMS_EPS = 1e-6
ROPE_THETA = 10000.0
LOG2_E = math.log2(math.e)

LANES = 128
BF16_SUBLANES = 16
VMEM_LIMIT_BYTES = 56 * 1024 * 1024

MLA_HEADS = 16
Q_LORA = 512
KV_LORA = 512
QK_NOPE = 128
QK_ROPE = 64
QK_HEAD = QK_NOPE + QK_ROPE
V_HEAD = 128
HEAD_PAD = 256

HGRN_HEADS = 16
HGRN_DK = 128
HGRN_DV = 128
CHUNK = 64
SUB = 16
HALF = SUB // 2

CONV_W = 3
F32_SUBLANES = 8
CONV_HALO = F32_SUBLANES


def _params(*sem):
    return pltpu.CompilerParams(dimension_semantics=sem, vmem_limit_bytes=VMEM_LIMIT_BYTES)


def _rms(x, g):
    return x * lax.rsqrt(jnp.mean(x * x, axis=-1, keepdims=True) + RMS_EPS) * g


def _sigmoid(x):
    return 1.0 / (1.0 + jnp.exp(-x))


def _dot(a, b):
    return jnp.dot(a, b, preferred_element_type=F32)


def _dot_nt(a, b):
    return lax.dot_general(a, b, (((1,), (1,)), ((), ())), preferred_element_type=F32)


def _dot_tn(a, b):
    return lax.dot_general(a, b, (((0,), (0,)), ((), ())), preferred_element_type=F32)


def _wspec(k, tn, layer, col, resident=False):
    mode = dict(pipeline_mode=pl.Buffered(1)) if resident else {}
    if layer is None:
        return pl.BlockSpec((k, tn), lambda *g: (0, col(*g)), **mode)
    return pl.BlockSpec((None, k, tn), lambda *g: (layer, 0, col(*g)), **mode)


def _pallas(body, ins, *, grid, in_specs, out_specs, out_shape, sem, name, scratch_shapes=(), casts=()):
    single = not isinstance(out_shape, (list, tuple))
    out_specs = [out_specs] if single else list(out_specs)
    out_shape = [out_shape] if single else list(out_shape)
    n_in, n_out, n_cast = len(ins), len(out_shape), len(casts)
    steps = math.prod(grid)

    def step_of(*g):
        idx = 0
        for size, pos in zip(grid, g):
            idx = idx * size + pos
        return idx

    c_in, c_in_specs, c_out_specs, c_out_shape, c_shapes = [], [], [], [], []
    for stack, layer in casts:
        layers, rows, cols = stack.shape
        size = rows * cols
        view_cols = next(c for c in (cols, cols // 2, cols // 4, cols // 8)
                         if c % LANES == 0 and (size // c) % (steps * BF16_SUBLANES) == 0)
        view_rows = size // view_cols
        rb = view_rows // steps
        c_in.append(stack.reshape(layers, view_rows, view_cols))
        c_in_specs.append(pl.BlockSpec((None, rb, view_cols), lambda *g, layer=layer: (layer, step_of(*g), 0)))
        c_out_specs.append(pl.BlockSpec((rb, view_cols), lambda *g: (step_of(*g), 0)))
        c_out_shape.append(jax.ShapeDtypeStruct((view_rows, view_cols), BF16))
        c_shapes.append((rows, cols))

    def kernel(*refs):
        outs_at = n_in + n_cast
        copies_at = outs_at + n_out
        for src, dst in zip(refs[n_in:outs_at], refs[copies_at:copies_at + n_cast]):
            dst[...] = src[...].astype(BF16)
        body(*refs[:n_in], *refs[outs_at:copies_at], *refs[copies_at + n_cast:])

    res = pl.pallas_call(
        kernel,
        grid=grid,
        in_specs=[*in_specs, *c_in_specs],
        out_specs=[*out_specs, *c_out_specs],
        out_shape=[*out_shape, *c_out_shape],
        scratch_shapes=list(scratch_shapes),
        compiler_params=_params(*sem),
        name=name,
    )(*ins, *c_in)
    outs = res[0] if single else list(res[:n_out])
    return outs, [copy.reshape(shape) for copy, shape in zip(res[n_out:], c_shapes)]


def _matmul_kernel(*refs, norm, residual):
    it = iter(refs)
    x_ref = next(it)
    g_ref = next(it) if norm else None
    w_ref = next(it)
    r_ref = next(it) if residual else None
    o_ref = next(it)
    xn_ref = next(it) if norm else None

    if norm:
        @pl.when(pl.program_id(1) == 0)
        def _():
            xn_ref[...] = _rms(x_ref[...], g_ref[...]).astype(BF16)
        acc = _dot(xn_ref[...], w_ref[...])
    else:
        acc = _dot(x_ref[...], w_ref[...])
    if residual:
        acc = r_ref[...] + acc
    o_ref[...] = acc.astype(o_ref.dtype)


def _matmul(x, w, layer, *, gain=None, residual=None, out_dtype, tm, tn, name, casts=()):
    m, k = x.shape
    n = w.shape[-1]
    tm, tn = min(tm, m), min(tn, n)
    assert m % tm == 0 and n % tn == 0
    norm = gain is not None
    ins = [x]
    specs = [pl.BlockSpec((tm, k), lambda i, j: (i, 0))]
    if norm:
        ins.append(gain.reshape(1, k))
        specs.append(pl.BlockSpec((1, k), lambda i, j: (0, 0)))
    ins.append(w)
    specs.append(_wspec(k, tn, layer, lambda i, j: j, resident=tn == n))
    if residual is not None:
        ins.append(residual)
        specs.append(pl.BlockSpec((tm, tn), lambda i, j: (i, j)))
    return _pallas(
        functools.partial(_matmul_kernel, norm=norm, residual=residual is not None), ins,
        grid=(m // tm, n // tn),
        in_specs=specs,
        out_specs=pl.BlockSpec((tm, tn), lambda i, j: (i, j)),
        out_shape=jax.ShapeDtypeStruct((m, n), out_dtype),
        scratch_shapes=[pltpu.VMEM((tm, k), BF16)] if norm else [],
        sem=("parallel", "arbitrary"),
        name=name,
        casts=casts,
    )


def _ffn_in_kernel(x_ref, g_ref, wg_ref, wu_ref, cw_ref, cb_ref, o_ref, xn_ref, gate_ref, carry_ref,
                   *, tm, tiles_per_seq):
    i = pl.program_id(0)
    j = pl.program_id(1)

    @pl.when(j == 0)
    def _():
        xn_ref[...] = _rms(x_ref[...], g_ref[...]).astype(BF16)

    seq_start = i % tiles_per_seq == 0

    @pl.when(seq_start)
    def _():
        gate_ref[:CONV_HALO, :] = jnp.zeros((CONV_HALO, gate_ref.shape[1]), F32)

    @pl.when(jnp.logical_not(seq_start))
    def _():
        gate_ref[:CONV_HALO, :] = carry_ref[j]

    gate_ref[CONV_HALO:, :] = _dot(xn_ref[...], wg_ref[...])
    up = _dot(xn_ref[...], wu_ref[...])
    carry_ref[j] = gate_ref[tm:, :]
    cw = cw_ref[...]
    conv = cb_ref[...] + gate_ref[pl.ds(CONV_HALO - 2, tm), :] * cw[0:1, :]
    conv = conv + gate_ref[pl.ds(CONV_HALO - 1, tm), :] * cw[1:2, :]
    conv = conv + gate_ref[pl.ds(CONV_HALO, tm), :] * cw[2:3, :]
    o_ref[...] = (conv * _sigmoid(conv) * up).astype(o_ref.dtype)


def _ffn_in(h, gain, w_in, layer, conv_w, conv_b, *, seq, tm, tn, casts=()):
    m, k = h.shape
    d_ff = w_in.shape[-1] // 2
    tm = min(tm, seq)
    assert seq % tm == 0 and d_ff % tn == 0 and tm % CONV_HALO == 0
    nj = d_ff // tn
    return _pallas(
        functools.partial(_ffn_in_kernel, tm=tm, tiles_per_seq=seq // tm),
        (h, gain.reshape(1, k), w_in, w_in, conv_w, conv_b.reshape(1, d_ff)),
        grid=(m // tm, nj),
        in_specs=[
            pl.BlockSpec((tm, k), lambda i, j: (i, 0)),
            pl.BlockSpec((1, k), lambda i, j: (0, 0)),
            _wspec(k, tn, layer, lambda i, j: j),
            _wspec(k, tn, layer, lambda i, j: j + nj),
            pl.BlockSpec((CONV_W, tn), lambda i, j: (0, j)),
            pl.BlockSpec((1, tn), lambda i, j: (0, j)),
        ],
        out_specs=pl.BlockSpec((tm, tn), lambda i, j: (i, j)),
        out_shape=jax.ShapeDtypeStruct((m, d_ff), BF16),
        scratch_shapes=[pltpu.VMEM((tm, k), BF16), pltpu.VMEM((tm + CONV_HALO, tn), F32),
                        pltpu.VMEM((nj, CONV_HALO, tn), F32)],
        sem=("arbitrary", "arbitrary"),
        name="ffn_in",
        casts=casts,
    )


def _ple_kernel(h_ref, g_ref, wg_ref, p_ref, wp_ref, o_ref):
    h = h_ref[...]
    proj = _dot(p_ref[...].astype(BF16), wp_ref[...])
    gate = _dot(_rms(h, g_ref[...]).astype(BF16), wg_ref[...])
    o_ref[...] = h + proj * _sigmoid(gate)


def _ple(h, gain, w_gate, p, w_proj, layer, *, tm):
    m, k = h.shape
    n = w_gate.shape[-1]
    kp = p.shape[2]
    tm = min(tm, m)
    assert m % tm == 0 and n == k
    gate_layer = layer if w_gate.ndim == 3 else None
    return pl.pallas_call(
        _ple_kernel,
        grid=(m // tm,),
        in_specs=[
            pl.BlockSpec((tm, k), lambda i: (i, 0)),
            pl.BlockSpec((1, k), lambda i: (0, 0)),
            _wspec(k, n, gate_layer, lambda i: 0, resident=True),
            pl.BlockSpec((None, tm, kp), lambda i: (layer, i, 0)),
            _wspec(kp, n, layer, lambda i: 0, resident=True),
        ],
        out_specs=pl.BlockSpec((tm, n), lambda i: (i, 0)),
        out_shape=jax.ShapeDtypeStruct((m, n), F32),
        compiler_params=_params("parallel"),
        name="ple",
    )(h, gain.reshape(1, k), w_gate, p, w_proj)


def _rope_table_kernel(pos_ref, inv_ref, c_ref, s_ref):
    ang = pos_ref[...].astype(F32) * inv_ref[...]
    lane = lax.broadcasted_iota(jnp.int32, ang.shape, 1)
    c = jnp.cos(ang)
    s = jnp.sin(ang)
    half = QK_ROPE // 2
    c_ref[...] = jnp.where(lane < QK_ROPE, c, 0.0)
    s_ref[...] = jnp.where(lane < half, -s, jnp.where(lane < QK_ROPE, s, 0.0))


def _rope_tables(positions, *, tm):
    m = positions.size
    tm = min(tm, m)
    half = QK_ROPE // 2
    inv_freq = ROPE_THETA ** (-jnp.arange(0, QK_ROPE, 2, dtype=F32) / QK_ROPE)
    inv_row = jnp.concatenate([inv_freq, inv_freq, jnp.zeros((LANES - 2 * half,), F32)]).reshape(1, LANES)
    out = jax.ShapeDtypeStruct((m, LANES), F32)
    spec = pl.BlockSpec((tm, LANES), lambda i: (i, 0))
    return pl.pallas_call(
        _rope_table_kernel,
        grid=(m // tm,),
        in_specs=[pl.BlockSpec((tm, 1), lambda i: (i, 0)), pl.BlockSpec((1, LANES), lambda i: (0, 0))],
        out_specs=[spec, spec],
        out_shape=[out, out],
        compiler_params=_params("parallel"),
        name="rope_tables",
    )(positions.reshape(m, 1), inv_row)


def _mla_proj_kernel(h_ref, g_ref, wa_ref, qn_ref, kvn_ref, wuq_ref, wukv_ref, gq_ref, gk_ref,
                     c_ref, s_ref, q_out, k_out, v_out):
    xn = _rms(h_ref[...], g_ref[...]).astype(BF16)
    c = _dot(xn, wa_ref[...])
    cq = _rms(c[:, :Q_LORA], qn_ref[...]).astype(BF16)
    ckv = _rms(c[:, Q_LORA:Q_LORA + KV_LORA], kvn_ref[...]).astype(BF16)
    k_pe = c[:, Q_LORA + KV_LORA:]
    cos_t, sin_t = c_ref[...], s_ref[...]

    def rope(r):
        return r * cos_t + pltpu.roll(r, LANES // 2, 1) * sin_t

    def row_total(sq):
        return jnp.sum(sq, axis=-1, keepdims=True)

    gq = gq_ref[...]
    gk = gk_ref[...]
    k_pe_sq = row_total(0.5 * (k_pe * k_pe))
    k_rope = rope(k_pe * gk[:, QK_NOPE:])

    for hd in range(MLA_HEADS):
        lo = hd * HEAD_PAD
        mid = lo + QK_NOPE
        hi = lo + HEAD_PAD
        qh = _dot(cq, wuq_ref[:, lo:hi])
        q_nope, q_rope = qh[:, :QK_NOPE], qh[:, QK_NOPE:]
        q_sq = row_total(q_nope * q_nope + 0.5 * (q_rope * q_rope))
        inv = lax.rsqrt(q_sq * (1.0 / QK_HEAD) + RMS_EPS)
        q_out[:, lo:mid] = (q_nope * inv * gq[:, :QK_NOPE]).astype(BF16)
        q_out[:, mid:hi] = rope(q_rope * inv * gq[:, QK_NOPE:]).astype(BF16)

        kvh = _dot(ckv, wukv_ref[:, lo:hi])
        kn = kvh[:, :QK_NOPE]
        inv_k = lax.rsqrt((row_total(kn * kn) + k_pe_sq) * (1.0 / QK_HEAD) + RMS_EPS)
        k_out[:, lo:mid] = (kn * inv_k * gk[:, :QK_NOPE]).astype(BF16)
        k_out[:, mid:hi] = (k_rope * inv_k).astype(BF16)
        v_out[:, hd * V_HEAD:(hd + 1) * V_HEAD] = kvh[:, QK_NOPE:].astype(BF16)


def _mla_proj(h, gain, w_a, q_norm, w_uq_pad, kv_norm, w_ukv, layer, gq_pad, gk_pad, tables, *, tm, casts=()):
    m, k = h.shape
    tm = min(tm, m)
    assert m % tm == 0
    a_dim = w_a.shape[2]
    nq = MLA_HEADS * HEAD_PAD
    const = lambda i: (0, 0)
    row = lambda i: (i, 0)
    return _pallas(
        _mla_proj_kernel,
        (h, gain.reshape(1, k), w_a, q_norm.reshape(1, Q_LORA), kv_norm.reshape(1, KV_LORA),
         w_uq_pad, w_ukv, gq_pad, gk_pad, *tables),
        grid=(m // tm,),
        in_specs=[
            pl.BlockSpec((tm, k), row),
            pl.BlockSpec((1, k), const),
            _wspec(k, a_dim, layer, lambda i: 0, resident=True),
            pl.BlockSpec((1, Q_LORA), const),
            pl.BlockSpec((1, KV_LORA), const),
            _wspec(Q_LORA, nq, layer, lambda i: 0, resident=True),
            _wspec(KV_LORA, nq, layer, lambda i: 0, resident=True),
            pl.BlockSpec((1, HEAD_PAD), const),
            pl.BlockSpec((1, HEAD_PAD), const),
            pl.BlockSpec((tm, LANES), row),
            pl.BlockSpec((tm, LANES), row),
        ],
        out_specs=[pl.BlockSpec((tm, nq), row), pl.BlockSpec((tm, nq), row),
                   pl.BlockSpec((tm, MLA_HEADS * V_HEAD), row)],
        out_shape=[jax.ShapeDtypeStruct((m, nq), BF16), jax.ShapeDtypeStruct((m, nq), BF16),
                   jax.ShapeDtypeStruct((m, MLA_HEADS * V_HEAD), BF16)],
        sem=("parallel",),
        name="mla_proj",
        casts=casts,
    )


def _attn_kernel(q_ref, k_ref, v_ref, o_ref, *, blk):
    seq = q_ref.shape[0]
    row = lax.broadcasted_iota(jnp.int32, (blk, blk), 0)
    col = lax.broadcasted_iota(jnp.int32, (blk, blk), 1)
    causal = col <= row
    for qi in range(seq // blk):
        q = q_ref[qi * blk:(qi + 1) * blk, :]
        m_run = jnp.full((blk, 1), -jnp.inf, F32)
        l_run = jnp.zeros((blk, 1), F32)
        acc = jnp.zeros((blk, V_HEAD), F32)
        for ki in range(qi + 1):
            s = _dot_nt(q, k_ref[ki * blk:(ki + 1) * blk, :])
            if ki == qi:
                s = jnp.where(causal, s, -jnp.inf)
            m_new = jnp.maximum(m_run, jnp.max(s, axis=-1, keepdims=True))
            p = jnp.exp2(s - m_new)
            alpha = jnp.exp2(m_run - m_new)
            l_run = alpha * l_run + jnp.sum(p, axis=-1, keepdims=True)
            acc = alpha * acc + _dot(p.astype(BF16), v_ref[ki * blk:(ki + 1) * blk, :])
            m_run = m_new
        o_ref[qi * blk:(qi + 1) * blk, :] = (acc / l_run).astype(o_ref.dtype)


def _attention(q, k, v, *, batch, seq, blk, casts=()):
    m = q.shape[0]
    blk = min(blk, seq)
    assert seq % blk == 0
    return _pallas(
        functools.partial(_attn_kernel, blk=blk), (q, k, v),
        grid=(batch, MLA_HEADS),
        in_specs=[
            pl.BlockSpec((seq, HEAD_PAD), lambda b, h: (b, h)),
            pl.BlockSpec((seq, HEAD_PAD), lambda b, h: (b, h)),
            pl.BlockSpec((seq, V_HEAD), lambda b, h: (b, h)),
        ],
        out_specs=pl.BlockSpec((seq, V_HEAD), lambda b, h: (b, h)),
        out_shape=jax.ShapeDtypeStruct((m, MLA_HEADS * V_HEAD), BF16),
        sem=("parallel", "parallel"),
        name="mla_attention",
        casts=casts,
    )


def _norm_kernel(x_ref, g_ref, o_ref):
    o_ref[...] = _rms(x_ref[...], g_ref[...]).astype(BF16)


def _rmsnorm(x, gain, *, tm):
    m, k = x.shape
    tm = min(tm, m)
    assert m % tm == 0
    return pl.pallas_call(
        _norm_kernel,
        grid=(m // tm,),
        in_specs=[pl.BlockSpec((tm, k), lambda i: (i, 0)), pl.BlockSpec((1, k), lambda i: (0, 0))],
        out_specs=pl.BlockSpec((tm, k), lambda i: (i, 0)),
        out_shape=jax.ShapeDtypeStruct((m, k), BF16),
        compiler_params=_params("parallel"),
        name="rmsnorm",
    )(x, gain.reshape(1, k))


def _hgrn_prep(z_ref, slot, lb, tri, qs_ref, cs_ref, bs_ref, qi_ref, kd_ref, eb_ref, *, chunks):
    width = lb.shape[1]
    for c in range(chunks):
        rows = slice(c * CHUNK, (c + 1) * CHUNK)
        ff = z_ref[slot, rows, width:2 * width]
        e_abs = jnp.exp(-jnp.abs(ff))
        r_abs = 1.0 / (1.0 + e_abs)
        f = lb + (1.0 - lb) * jnp.where(ff >= 0.0, r_abs, e_abs * r_abs)
        log_f = jnp.log2(f)
        log_k = jnp.log2(1.0 - lb) - (jnp.maximum(ff, 0.0) * LOG2_E + jnp.log2(1.0 + e_abs))
        q_raw = z_ref[slot, rows, 0:width]
        q = q_raw * _sigmoid(q_raw)
        hi = log_f.astype(BF16)
        r1 = log_f - hi.astype(F32)
        mid = r1.astype(BF16)
        lo = (r1 - mid.astype(F32)).astype(BF16)
        parts = _dot(tri, jnp.concatenate([hi, mid, lo], axis=1))
        b = parts[:, :width] + parts[:, width:2 * width] + parts[:, 2 * width:]
        b_last = b[CHUNK - 1:CHUNK, :]
        c_dec = b - log_k
        qs_ref[rows, :] = q
        cs_ref[rows, :] = c_dec
        bs_ref[rows, :] = b
        qi_ref[rows, :] = (q * jnp.exp2(b)).astype(BF16)
        kd_ref[rows, :] = jnp.exp2(b_last - c_dec).astype(BF16)
        eb_ref[c] = jnp.exp2(b_last)


def _hgrn_intra(q, c_dec, b):
    lane = lax.broadcasted_iota(jnp.int32, (HALF, CHUNK), 1)
    srow = lax.broadcasted_iota(jnp.int32, (CHUNK, HGRN_DK), 0)
    a_rows = []
    for blk in range(CHUNK // SUB):
        r0 = blk * SUB
        halves = []
        for hf in range(2):
            t0 = r0 + hf * HALF
            q_t = q[t0:t0 + HALF, :]
            b_t = b[t0:t0 + HALF, :]
            a_half = jnp.zeros((HALF, CHUNK), F32)
            for sl in range((hf + 1) * HALF):
                s = r0 + sl
                col = jnp.sum(q_t * jnp.exp2(b_t - c_dec[s:s + 1, :]), axis=-1, keepdims=True)
                a_half = jnp.where(lane == s, col, a_half)
            halves.append(a_half)
        a_blk = jnp.concatenate(halves, axis=0)
        if blk > 0:
            ref_b = b[r0 - 1:r0, :]
            q_f = (q[r0:r0 + SUB, :] * jnp.exp2(b[r0:r0 + SUB, :] - ref_b)).astype(BF16)
            k_f = jnp.where(srow < r0, jnp.exp2(ref_b - c_dec), 0.0).astype(BF16)
            a_blk = a_blk + _dot_nt(q_f, k_f)
        a_rows.append(a_blk)
    t_idx = lax.broadcasted_iota(jnp.int32, (CHUNK, CHUNK), 0)
    s_idx = lax.broadcasted_iota(jnp.int32, (CHUNK, CHUNK), 1)
    return jnp.where(s_idx <= t_idx, jnp.concatenate(a_rows, axis=0), 0.0)


def _hgrn_kernel(xn_ref, x0_ref, w0_ref, w1_ref, w2_ref, w3_ref, lb_ref, on_ref, o_ref,
                 z_ref, st_ref, qs_ref, cs_ref, bs_ref, qi_ref, kd_ref, eb_ref, a_ref, *, heads, chunks, nt):
    step = pl.program_id(0)
    w_refs = (w0_ref, w1_ref, w2_ref, w3_ref)
    width = heads * HGRN_DK
    slot = step % 2

    @pl.when(step == 0)
    def _():
        x0 = x0_ref[...]
        for part, w_ref in enumerate(w_refs):
            z_ref[0, :, part * width:(part + 1) * width] = _dot(x0, w_ref[...])

    @pl.when(step % nt == 0)
    def _():
        st_ref[...] = jnp.zeros_like(st_ref)

    r = lax.broadcasted_iota(jnp.int32, (CHUNK, CHUNK), 0)
    c = lax.broadcasted_iota(jnp.int32, (CHUNK, CHUNK), 1)
    tri = (c <= r).astype(BF16)

    def build_a(ci):
        rows = slice(ci * CHUNK, (ci + 1) * CHUNK)
        for hd in range(heads):
            cols = slice(hd * HGRN_DK, (hd + 1) * HGRN_DK)
            a_ref[ci % 2, hd] = _hgrn_intra(qs_ref[rows, cols], cs_ref[rows, cols], bs_ref[rows, cols]).astype(BF16)

    def finish(ci):
        rows = slice(ci * CHUNK, (ci + 1) * CHUNK)
        for hd in range(heads):
            cols = slice(hd * HGRN_DK, (hd + 1) * HGRN_DK)
            v16 = z_ref[slot, rows, 2 * width + hd * HGRN_DK:2 * width + (hd + 1) * HGRN_DK].astype(BF16)
            st = st_ref[hd]
            o = _dot_nt(qi_ref[rows, cols], st.astype(BF16)) + _dot(a_ref[ci % 2, hd], v16)
            st_ref[hd] = st * eb_ref[ci, :, cols] + _dot_tn(v16, kd_ref[rows, cols])
            o = o * lax.rsqrt(jnp.mean(o * o, axis=-1, keepdims=True) + RMS_EPS) * on_ref[:, cols]
            gate = z_ref[slot, rows, 3 * width + hd * HGRN_DK:3 * width + (hd + 1) * HGRN_DK]
            o_ref[rows, cols] = (o * (gate * _sigmoid(gate))).astype(o_ref.dtype)

    def project_next(part):
        z_ref[1 - slot, :, part * width:(part + 1) * width] = _dot(xn_ref[...], w_refs[part][...])

    project_next(0)
    _hgrn_prep(z_ref, slot, lb_ref[...], tri, qs_ref, cs_ref, bs_ref, qi_ref, kd_ref, eb_ref, chunks=chunks)
    build_a(0)
    rest = len(w_refs) - 1
    starts = {(i * chunks + rest // 2) // rest: i + 1 for i in range(rest)}
    for ci in range(chunks):
        if ci in starts:
            project_next(starts[ci])
        finish(ci)
        if ci + 1 < chunks:
            build_a(ci + 1)


def _hgrn_mixer_core(xn, w_in, lb, o_gain, *, batch, seq, heads, tb, casts=()):
    m, k = xn.shape
    fdim = HGRN_HEADS * HGRN_DK
    tb = min(tb, seq)
    width = heads * HGRN_DK
    groups = HGRN_HEADS // heads
    nt = seq // tb
    chunks = tb // CHUNK
    n_parts = 4
    assert seq % tb == 0 and tb % CHUNK == 0 and HGRN_HEADS % heads == 0 and chunks % n_parts == 0 and nt >= 2
    steps = batch * groups * nt

    def rows_of(s):
        return (s // (nt * groups)) * nt + s % nt

    def group_of(s):
        return (s // nt) % groups

    def nxt(s):
        return jnp.minimum(s + 1, steps - 1)

    def wspec(part):
        return pl.BlockSpec((k, width), lambda s: (0, part * groups + group_of(nxt(s))))

    return _pallas(
        functools.partial(_hgrn_kernel, heads=heads, chunks=chunks, nt=nt),
        (xn, xn, w_in, w_in, w_in, w_in, lb.reshape(1, fdim), o_gain.reshape(1, fdim)),
        grid=(steps,),
        in_specs=[pl.BlockSpec((tb, k), lambda s: (rows_of(nxt(s)), 0)),
                  pl.BlockSpec((tb, k), lambda s: (0, 0), pipeline_mode=pl.Buffered(1)),
                  wspec(0), wspec(1), wspec(2), wspec(3),
                  pl.BlockSpec((1, width), lambda s: (0, group_of(s))),
                  pl.BlockSpec((1, width), lambda s: (0, group_of(s)))],
        out_specs=pl.BlockSpec((tb, width), lambda s: (rows_of(s), group_of(s))),
        out_shape=jax.ShapeDtypeStruct((m, fdim), BF16),
        scratch_shapes=[
            pltpu.VMEM((2, tb, n_parts * width), F32),
            pltpu.VMEM((heads, HGRN_DV, HGRN_DK), F32),
            pltpu.VMEM((tb, width), F32),
            pltpu.VMEM((tb, width), F32),
            pltpu.VMEM((tb, width), F32),
            pltpu.VMEM((tb, width), BF16),
            pltpu.VMEM((tb, width), BF16),
            pltpu.VMEM((chunks, 1, width), F32),
            pltpu.VMEM((2, heads, CHUNK, CHUNK), BF16),
        ],
        sem=("arbitrary",),
        name="hgrn_mixer",
        casts=casts,
    )


def _with_swapped_rope(t):
    half = QK_ROPE // 2
    return jnp.concatenate([t, t[..., -half:], t[..., -QK_ROPE:-half]], axis=-1)


def _mla_layer(h, tables, layer, mix_gain, w_a_pad, q_norm, w_uq_pad, kv_norm, w_ukv, gain_q, gain_k, w_o,
               *, batch, seq, casts=()):
    gq_pad = _with_swapped_rope(gain_q * (LOG2_E / math.sqrt(QK_HEAD))).reshape(1, HEAD_PAD)
    gk_pad = _with_swapped_rope(gain_k).reshape(1, HEAD_PAD)
    (q, k, v), (w_o16,) = _mla_proj(h, mix_gain, w_a_pad, q_norm, w_uq_pad, kv_norm, w_ukv, layer, gq_pad, gk_pad,
                                    tables, tm=512, casts=[(w_o, layer)])
    o, copies = _attention(q, k, v, batch=batch, seq=seq, blk=256, casts=casts)
    h, _ = _matmul(o, w_o16, None, residual=h, out_dtype=F32, tm=1024, tn=2048, name="mla_out")
    return h, copies


def _hgrn_layer(h, layer, mix_gain, lb, w_in, o_gain, w_o, *, batch, seq, casts=()):
    xn = _rmsnorm(h, mix_gain, tm=1024)
    o, copies = _hgrn_mixer_core(xn, w_in, lb, o_gain, batch=batch, seq=seq, heads=4, tb=512,
                                 casts=[(w_o, layer), *casts])
    h, _ = _matmul(o, copies[0], None, residual=h, out_dtype=F32, tm=1024, tn=2048, name="hgrn_out")
    return h, copies[1:]


def kernel(x, p, positions, mix_norm, ffn_norm, ple_norm, mla_w_a, mla_q_norm, mla_w_uq, mla_kv_norm, mla_w_ukv, mla_qk_gain_q, mla_qk_gain_k, mla_w_o, hgrn_lb_logits, hgrn_w_in, hgrn_o_norm, hgrn_w_o, ffn_w_in, ffn_conv_w, ffn_conv_b, ffn_w_down, ple_w_proj, ple_w_gate):
    batch, seq, d_model = x.shape
    depth = p.shape[0]
    m = batch * seq
    tables = _rope_tables(positions, tm=1024)
    sm = jax.nn.softmax(hgrn_lb_logits.astype(F32), axis=0)
    lower_bounds = jnp.cumsum(sm, axis=0) - sm[0]

    n_mla = mla_w_a.shape[0]
    w_a_pad = _with_swapped_rope(mla_w_a).astype(BF16)
    w_uq_pad = _with_swapped_rope(mla_w_uq.reshape(n_mla, Q_LORA, MLA_HEADS, QK_HEAD))
    w_uq_pad = w_uq_pad.reshape(n_mla, Q_LORA, MLA_HEADS * HEAD_PAD).astype(BF16)
    w_ukv = mla_w_ukv.astype(BF16)
    w_ple_proj = ple_w_proj.astype(BF16)
    p_rows = p.reshape(depth, m, -1)

    h = x.reshape(m, d_model)
    w_hgrn_in = None
    for i in range(depth):
        j = i // 2
        if i % 2 == 0:
            h, (w_ffn_in,) = _mla_layer(h, tables, j, mix_norm[i], w_a_pad, mla_q_norm[j], w_uq_pad, mla_kv_norm[j],
                                        w_ukv, mla_qk_gain_q[j], mla_qk_gain_k[j], mla_w_o, batch=batch, seq=seq,
                                        casts=[(ffn_w_in, i)])
        else:
            h, (w_ffn_in,) = _hgrn_layer(h, j, mix_norm[i], lower_bounds[i], w_hgrn_in, hgrn_o_norm[j], hgrn_w_o,
                                         batch=batch, seq=seq, casts=[(ffn_w_in, i)])
        act, (w_ffn_down,) = _ffn_in(h, ffn_norm[i], w_ffn_in, None, ffn_conv_w[i], ffn_conv_b[i],
                                     seq=seq, tm=1024, tn=512, casts=[(ffn_w_down, i)])
        down_casts = [(ple_w_gate, i)]
        if i + 1 < depth and (i + 1) % 2 == 1:
            down_casts.append((hgrn_w_in, (i + 1) // 2))
        h, copies = _matmul(act, w_ffn_down, None, residual=h, out_dtype=F32, tm=1024, tn=512, name="ffn_down",
                            casts=down_casts)
        w_hgrn_in = copies[1] if len(copies) > 1 else None
        h = _ple(h, ple_norm[i], copies[0], p_rows, w_ple_proj, i, tm=1024)
    return h.reshape(batch, seq, d_model)
```

```python
import functools
import math

import jax
import jax.numpy as jnp
from jax import lax
from jax.experimental import pallas as pl
from jax.experimental.pallas import tpu as pltpu

F32 = jnp.float32
BF16 = jnp.bfloat16

RMS_EPS = 1e-6
ROPE_THETA = 10000.0
LOG2_E = math.log2(math.e)

LANES = 128
BF16_SUBLANES = 16
VMEM_LIMIT_BYTES = 56 * 1024 * 1024

MLA_HEADS = 16
Q_LORA = 512
KV_LORA = 512
QK_NOPE = 128
QK_ROPE = 64
QK_HEAD = QK_NOPE + QK_ROPE
V_HEAD = 128
HEAD_PAD = 256

HGRN_HEADS = 16
HGRN_DK = 128
HGRN_DV = 128
CHUNK = 64
SUB = 16
HALF = SUB // 2

CONV_W = 3
F32_SUBLANES = 8
CONV_HALO = F32_SUBLANES


def _params(*sem):
    return pltpu.CompilerParams(dimension_semantics=sem, vmem_limit_bytes=VMEM_LIMIT_BYTES)


def _rms(x, g):
    return x * lax.rsqrt(jnp.mean(x * x, axis=-1, keepdims=True) + RMS_EPS) * g


def _sigmoid(x):
    return 1.0 / (1.0 + jnp.exp(-x))


def _dot(a, b):
    return jnp.dot(a, b, preferred_element_type=F32)


def _dot_nt(a, b):
    return lax.dot_general(a, b, (((1,), (1,)), ((), ())), preferred_element_type=F32)


def _dot_tn(a, b):
    return lax.dot_general(a, b, (((0,), (0,)), ((), ())), preferred_element_type=F32)


def _wspec(k, tn, layer, col, resident=False):
    mode = dict(pipeline_mode=pl.Buffered(1)) if resident else {}
    if layer is None:
        return pl.BlockSpec((k, tn), lambda *g: (0, col(*g)), **mode)
    return pl.BlockSpec((None, k, tn), lambda *g: (layer, 0, col(*g)), **mode)


def _pallas(body, ins, *, grid, in_specs, out_specs, out_shape, sem, name, scratch_shapes=(), casts=()):
    single = not isinstance(out_shape, (list, tuple))
    out_specs = [out_specs] if single else list(out_specs)
    out_shape = [out_shape] if single else list(out_shape)
    n_in, n_out, n_cast = len(ins), len(out_shape), len(casts)
    steps = math.prod(grid)

    def step_of(*g):
        idx = 0
        for size, pos in zip(grid, g):
            idx = idx * size + pos
        return idx

    c_in, c_in_specs, c_out_specs, c_out_shape, c_shapes = [], [], [], [], []
    for stack, layer in casts:
        layers, rows, cols = stack.shape
        size = rows * cols
        view_cols = next(c for c in (cols, cols // 2, cols // 4, cols // 8)
                         if c % LANES == 0 and (size // c) % (steps * BF16_SUBLANES) == 0)
        view_rows = size // view_cols
        rb = view_rows // steps
        c_in.append(stack.reshape(layers, view_rows, view_cols))
        c_in_specs.append(pl.BlockSpec((None, rb, view_cols), lambda *g, layer=layer: (layer, step_of(*g), 0)))
        c_out_specs.append(pl.BlockSpec((rb, view_cols), lambda *g: (step_of(*g), 0)))
        c_out_shape.append(jax.ShapeDtypeStruct((view_rows, view_cols), BF16))
        c_shapes.append((rows, cols))

    def kernel(*refs):
        outs_at = n_in + n_cast
        copies_at = outs_at + n_out
        for src, dst in zip(refs[n_in:outs_at], refs[copies_at:copies_at + n_cast]):
            dst[...] = src[...].astype(BF16)
        body(*refs[:n_in], *refs[outs_at:copies_at], *refs[copies_at + n_cast:])

    res = pl.pallas_call(
        kernel,
        grid=grid,
        in_specs=[*in_specs, *c_in_specs],
        out_specs=[*out_specs, *c_out_specs],
        out_shape=[*out_shape, *c_out_shape],
        scratch_shapes=list(scratch_shapes),
        compiler_params=_params(*sem),
        name=name,
    )(*ins, *c_in)
    outs = res[0] if single else list(res[:n_out])
    return outs, [copy.reshape(shape) for copy, shape in zip(res[n_out:], c_shapes)]


def _matmul_kernel(*refs, norm, residual):
    it = iter(refs)
    x_ref = next(it)
    g_ref = next(it) if norm else None
    w_ref = next(it)
    r_ref = next(it) if residual else None
    o_ref = next(it)
    xn_ref = next(it) if norm else None

    if norm:
        @pl.when(pl.program_id(1) == 0)
        def _():
            xn_ref[...] = _rms(x_ref[...], g_ref[...]).astype(BF16)
        acc = _dot(xn_ref[...], w_ref[...])
    else:
        acc = _dot(x_ref[...], w_ref[...])
    if residual:
        acc = r_ref[...] + acc
    o_ref[...] = acc.astype(o_ref.dtype)


def _matmul(x, w, layer, *, gain=None, residual=None, out_dtype, tm, tn, name, casts=()):
    m, k = x.shape
    n = w.shape[-1]
    tm, tn = min(tm, m), min(tn, n)
    assert m % tm == 0 and n % tn == 0
    norm = gain is not None
    ins = [x]
    specs = [pl.BlockSpec((tm, k), lambda i, j: (i, 0))]
    if norm:
        ins.append(gain.reshape(1, k))
        specs.append(pl.BlockSpec((1, k), lambda i, j: (0, 0)))
    ins.append(w)
    specs.append(_wspec(k, tn, layer, lambda i, j: j, resident=tn == n))
    if residual is not None:
        ins.append(residual)
        specs.append(pl.BlockSpec((tm, tn), lambda i, j: (i, j)))
    return _pallas(
        functools.partial(_matmul_kernel, norm=norm, residual=residual is not None), ins,
        grid=(m // tm, n // tn),
        in_specs=specs,
        out_specs=pl.BlockSpec((tm, tn), lambda i, j: (i, j)),
        out_shape=jax.ShapeDtypeStruct((m, n), out_dtype),
        scratch_shapes=[pltpu.VMEM((tm, k), BF16)] if norm else [],
        sem=("parallel", "arbitrary"),
        name=name,
        casts=casts,
    )


def _ffn_in_kernel(x_ref, g_ref, wg_ref, wu_ref, cw_ref, cb_ref, o_ref, xn_ref, gate_ref, carry_ref,
                   *, tm, tiles_per_seq):
    i = pl.program_id(0)
    j = pl.program_id(1)

    @pl.when(j == 0)
    def _():
        xn_ref[...] = _rms(x_ref[...], g_ref[...]).astype(BF16)

    seq_start = i % tiles_per_seq == 0

    @pl.when(seq_start)
    def _():
        gate_ref[:CONV_HALO, :] = jnp.zeros((CONV_HALO, gate_ref.shape[1]), F32)

    @pl.when(jnp.logical_not(seq_start))
    def _():
        gate_ref[:CONV_HALO, :] = carry_ref[j]

    gate_ref[CONV_HALO:, :] = _dot(xn_ref[...], wg_ref[...])
    up = _dot(xn_ref[...], wu_ref[...])
    carry_ref[j] = gate_ref[tm:, :]
    cw = cw_ref[...]
    conv = cb_ref[...] + gate_ref[pl.ds(CONV_HALO - 2, tm), :] * cw[0:1, :]
    conv = conv + gate_ref[pl.ds(CONV_HALO - 1, tm), :] * cw[1:2, :]
    conv = conv + gate_ref[pl.ds(CONV_HALO, tm), :] * cw[2:3, :]
    o_ref[...] = (conv * _sigmoid(conv) * up).astype(o_ref.dtype)


def _ffn_in(h, gain, w_in, layer, conv_w, conv_b, *, seq, tm, tn, casts=()):
    m, k = h.shape
    d_ff = w_in.shape[-1] // 2
    tm = min(tm, seq)
    assert seq % tm == 0 and d_ff % tn == 0 and tm % CONV_HALO == 0
    nj = d_ff // tn
    return _pallas(
        functools.partial(_ffn_in_kernel, tm=tm, tiles_per_seq=seq // tm),
        (h, gain.reshape(1, k), w_in, w_in, conv_w, conv_b.reshape(1, d_ff)),
        grid=(m // tm, nj),
        in_specs=[
            pl.BlockSpec((tm, k), lambda i, j: (i, 0)),
            pl.BlockSpec((1, k), lambda i, j: (0, 0)),
            _wspec(k, tn, layer, lambda i, j: j),
            _wspec(k, tn, layer, lambda i, j: j + nj),
            pl.BlockSpec((CONV_W, tn), lambda i, j: (0, j)),
            pl.BlockSpec((1, tn), lambda i, j: (0, j)),
        ],
        out_specs=pl.BlockSpec((tm, tn), lambda i, j: (i, j)),
        out_shape=jax.ShapeDtypeStruct((m, d_ff), BF16),
        scratch_shapes=[pltpu.VMEM((tm, k), BF16), pltpu.VMEM((tm + CONV_HALO, tn), F32),
                        pltpu.VMEM((nj, CONV_HALO, tn), F32)],
        sem=("arbitrary", "arbitrary"),
        name="ffn_in",
        casts=casts,
    )


def _ple_kernel(h_ref, g_ref, wg_ref, p_ref, wp_ref, o_ref):
    h = h_ref[...]
    proj = _dot(p_ref[...].astype(BF16), wp_ref[...])
    gate = _dot(_rms(h, g_ref[...]).astype(BF16), wg_ref[...])
    o_ref[...] = h + proj * _sigmoid(gate)


def _ple(h, gain, w_gate, p, w_proj, layer, *, tm):
    m, k = h.shape
    n = w_gate.shape[-1]
    kp = p.shape[2]
    tm = min(tm, m)
    assert m % tm == 0 and n == k
    gate_layer = layer if w_gate.ndim == 3 else None
    return pl.pallas_call(
        _ple_kernel,
        grid=(m // tm,),
        in_specs=[
            pl.BlockSpec((tm, k), lambda i: (i, 0)),
            pl.BlockSpec((1, k), lambda i: (0, 0)),
            _wspec(k, n, gate_layer, lambda i: 0, resident=True),
            pl.BlockSpec((None, tm, kp), lambda i: (layer, i, 0)),
            _wspec(kp, n, layer, lambda i: 0, resident=True),
        ],
        out_specs=pl.BlockSpec((tm, n), lambda i: (i, 0)),
        out_shape=jax.ShapeDtypeStruct((m, n), F32),
        compiler_params=_params("parallel"),
        name="ple",
    )(h, gain.reshape(1, k), w_gate, p, w_proj)


def _rope_table_kernel(pos_ref, inv_ref, c_ref, s_ref):
    ang = pos_ref[...].astype(F32) * inv_ref[...]
    lane = lax.broadcasted_iota(jnp.int32, ang.shape, 1)
    c = jnp.cos(ang)
    s = jnp.sin(ang)
    half = QK_ROPE // 2
    c_ref[...] = jnp.where(lane < QK_ROPE, c, 0.0)
    s_ref[...] = jnp.where(lane < half, -s, jnp.where(lane < QK_ROPE, s, 0.0))


def _rope_tables(positions, *, tm):
    m = positions.size
    tm = min(tm, m)
    half = QK_ROPE // 2
    inv_freq = ROPE_THETA ** (-jnp.arange(0, QK_ROPE, 2, dtype=F32) / QK_ROPE)
    inv_row = jnp.concatenate([inv_freq, inv_freq, jnp.zeros((LANES - 2 * half,), F32)]).reshape(1, LANES)
    out = jax.ShapeDtypeStruct((m, LANES), F32)
    spec = pl.BlockSpec((tm, LANES), lambda i: (i, 0))
    return pl.pallas_call(
        _rope_table_kernel,
        grid=(m // tm,),
        in_specs=[pl.BlockSpec((tm, 1), lambda i: (i, 0)), pl.BlockSpec((1, LANES), lambda i: (0, 0))],
        out_specs=[spec, spec],
        out_shape=[out, out],
        compiler_params=_params("parallel"),
        name="rope_tables",
    )(positions.reshape(m, 1), inv_row)


def _mla_proj_kernel(h_ref, h0_ref, g_ref, wa_ref, qn_ref, kvn_ref, wuq_ref, wukv_ref, gq_ref, gk_ref,
                     c_ref, s_ref, q_out, k_out, v_out, qraw_ref, kvraw_ref, kpe_ref):
    step = pl.program_id(0)

    def project(x, dst):
        xn = _rms(x, g_ref[...]).astype(BF16)
        c = _dot(xn, wa_ref[...])
        cq = _rms(c[:, :Q_LORA], qn_ref[...]).astype(BF16)
        ckv = _rms(c[:, Q_LORA:Q_LORA + KV_LORA], kvn_ref[...]).astype(BF16)
        kpe_ref[dst] = c[:, Q_LORA + KV_LORA:]
        qraw_ref[dst] = _dot(cq, wuq_ref[...])
        kvraw_ref[dst] = _dot(ckv, wukv_ref[...])

    def finish(slot):
        k_pe = kpe_ref[slot]
        cos_t, sin_t = c_ref[...], s_ref[...]

        def rope(r):
            return r * cos_t + pltpu.roll(r, LANES // 2, 1) * sin_t

        def row_total(sq):
            return jnp.sum(sq, axis=-1, keepdims=True)

        gq = gq_ref[...]
        gk = gk_ref[...]
        k_pe_sq = row_total(0.5 * (k_pe * k_pe))
        k_rope = rope(k_pe * gk[:, QK_NOPE:])

        for hd in range(MLA_HEADS):
            lo = hd * HEAD_PAD
            mid = lo + QK_NOPE
            hi = lo + HEAD_PAD
            q_nope, q_rope = qraw_ref[slot, :, lo:mid], qraw_ref[slot, :, mid:hi]
            q_sq = row_total(q_nope * q_nope + 0.5 * (q_rope * q_rope))
            inv = lax.rsqrt(q_sq * (1.0 / QK_HEAD) + RMS_EPS)
            q_out[:, lo:mid] = (q_nope * inv * gq[:, :QK_NOPE]).astype(BF16)
            q_out[:, mid:hi] = rope(q_rope * inv * gq[:, QK_NOPE:]).astype(BF16)

            kn = kvraw_ref[slot, :, lo:mid]
            inv_k = lax.rsqrt((row_total(kn * kn) + k_pe_sq) * (1.0 / QK_HEAD) + RMS_EPS)
            k_out[:, lo:mid] = (kn * inv_k * gk[:, :QK_NOPE]).astype(BF16)
            k_out[:, mid:hi] = (k_rope * inv_k).astype(BF16)
            v_out[:, hd * V_HEAD:(hd + 1) * V_HEAD] = kvraw_ref[slot, :, mid:hi].astype(BF16)

    @pl.when(step == 0)
    def _():
        project(h0_ref[...], 0)

    for cur in range(2):
        @pl.when(step % 2 == cur)
        def _(cur=cur):
            project(h_ref[...], 1 - cur)
            finish(cur)


def _mla_proj(h, gain, w_a, q_norm, w_uq_pad, kv_norm, w_ukv, layer, gq_pad, gk_pad, tables, *, tm, casts=()):
    m, k = h.shape
    tm = min(tm, m)
    assert m % tm == 0
    a_dim = w_a.shape[2]
    nq = MLA_HEADS * HEAD_PAD
    steps = m // tm
    const = lambda i: (0, 0)
    row = lambda i: (i, 0)
    return _pallas(
        _mla_proj_kernel,
        (h, h, gain.reshape(1, k), w_a, q_norm.reshape(1, Q_LORA), kv_norm.reshape(1, KV_LORA),
         w_uq_pad, w_ukv, gq_pad, gk_pad, *tables),
        grid=(steps,),
        in_specs=[
            pl.BlockSpec((tm, k), lambda i: (jnp.minimum(i + 1, steps - 1), 0)),
            pl.BlockSpec((tm, k), const, pipeline_mode=pl.Buffered(1)),
            pl.BlockSpec((1, k), const),
            _wspec(k, a_dim, layer, lambda i: 0, resident=True),
            pl.BlockSpec((1, Q_LORA), const),
            pl.BlockSpec((1, KV_LORA), const),
            _wspec(Q_LORA, nq, layer, lambda i: 0, resident=True),
            _wspec(KV_LORA, nq, layer, lambda i: 0, resident=True),
            pl.BlockSpec((1, HEAD_PAD), const),
            pl.BlockSpec((1, HEAD_PAD), const),
            pl.BlockSpec((tm, LANES), row),
            pl.BlockSpec((tm, LANES), row),
        ],
        out_specs=[pl.BlockSpec((tm, nq), row), pl.BlockSpec((tm, nq), row),
                   pl.BlockSpec((tm, MLA_HEADS * V_HEAD), row)],
        out_shape=[jax.ShapeDtypeStruct((m, nq), BF16), jax.ShapeDtypeStruct((m, nq), BF16),
                   jax.ShapeDtypeStruct((m, MLA_HEADS * V_HEAD), BF16)],
        scratch_shapes=[pltpu.VMEM((2, tm, nq), F32),
                        pltpu.VMEM((2, tm, nq), F32),
                        pltpu.VMEM((2, tm, LANES), F32)],
        sem=("arbitrary",),
        name="mla_proj",
        casts=casts,
    )


def _attn_kernel(q_ref, k_ref, v_ref, o_ref, *, blk):
    seq = q_ref.shape[0]
    row = lax.broadcasted_iota(jnp.int32, (blk, blk), 0)
    col = lax.broadcasted_iota(jnp.int32, (blk, blk), 1)
    causal = col <= row
    for qi in range(seq // blk):
        q = q_ref[qi * blk:(qi + 1) * blk, :]
        m_run = jnp.full((blk, 1), -jnp.inf, F32)
        l_run = jnp.zeros((blk, 1), F32)
        acc = jnp.zeros((blk, V_HEAD), F32)
        for ki in range(qi + 1):
            s = _dot_nt(q, k_ref[ki * blk:(ki + 1) * blk, :])
            if ki == qi:
                s = jnp.where(causal, s, -jnp.inf)
            m_new = jnp.maximum(m_run, jnp.max(s, axis=-1, keepdims=True))
            p = jnp.exp2(s - m_new)
            alpha = jnp.exp2(m_run - m_new)
            l_run = alpha * l_run + jnp.sum(p, axis=-1, keepdims=True)
            acc = alpha * acc + _dot(p.astype(BF16), v_ref[ki * blk:(ki + 1) * blk, :])
            m_run = m_new
        o_ref[qi * blk:(qi + 1) * blk, :] = (acc / l_run).astype(o_ref.dtype)


def _attention(q, k, v, *, batch, seq, blk, casts=()):
    m = q.shape[0]
    blk = min(blk, seq)
    assert seq % blk == 0
    return _pallas(
        functools.partial(_attn_kernel, blk=blk), (q, k, v),
        grid=(batch, MLA_HEADS),
        in_specs=[
            pl.BlockSpec((seq, HEAD_PAD), lambda b, h: (b, h)),
            pl.BlockSpec((seq, HEAD_PAD), lambda b, h: (b, h)),
            pl.BlockSpec((seq, V_HEAD), lambda b, h: (b, h)),
        ],
        out_specs=pl.BlockSpec((seq, V_HEAD), lambda b, h: (b, h)),
        out_shape=jax.ShapeDtypeStruct((m, MLA_HEADS * V_HEAD), BF16),
        sem=("parallel", "parallel"),
        name="mla_attention",
        casts=casts,
    )


def _norm_kernel(x_ref, g_ref, o_ref):
    o_ref[...] = _rms(x_ref[...], g_ref[...]).astype(BF16)


def _rmsnorm(x, gain, *, tm):
    m, k = x.shape
    tm = min(tm, m)
    assert m % tm == 0
    return pl.pallas_call(
        _norm_kernel,
        grid=(m // tm,),
        in_specs=[pl.BlockSpec((tm, k), lambda i: (i, 0)), pl.BlockSpec((1, k), lambda i: (0, 0))],
        out_specs=pl.BlockSpec((tm, k), lambda i: (i, 0)),
        out_shape=jax.ShapeDtypeStruct((m, k), BF16),
        compiler_params=_params("parallel"),
        name="rmsnorm",
    )(x, gain.reshape(1, k))


def _hgrn_prep(z_ref, slot, lb, tri, qs_ref, cs_ref, bs_ref, qi_ref, kd_ref, eb_ref, *, chunks):
    width = lb.shape[1]
    for c in range(chunks):
        rows = slice(c * CHUNK, (c + 1) * CHUNK)
        ff = z_ref[slot, rows, width:2 * width]
        e_abs = jnp.exp(-jnp.abs(ff))
        r_abs = 1.0 / (1.0 + e_abs)
        f = lb + (1.0 - lb) * jnp.where(ff >= 0.0, r_abs, e_abs * r_abs)
        log_f = jnp.log2(f)
        log_k = jnp.log2(1.0 - lb) - (jnp.maximum(ff, 0.0) * LOG2_E + jnp.log2(1.0 + e_abs))
        q_raw = z_ref[slot, rows, 0:width]
        q = q_raw * _sigmoid(q_raw)
        hi = log_f.astype(BF16)
        r1 = log_f - hi.astype(F32)
        mid = r1.astype(BF16)
        lo = (r1 - mid.astype(F32)).astype(BF16)
        parts = _dot(tri, jnp.concatenate([hi, mid, lo], axis=1))
        b = parts[:, :width] + parts[:, width:2 * width] + parts[:, 2 * width:]
        b_last = b[CHUNK - 1:CHUNK, :]
        c_dec = b - log_k
        qs_ref[rows, :] = q
        cs_ref[rows, :] = c_dec
        bs_ref[rows, :] = b
        qi_ref[rows, :] = (q * jnp.exp2(b)).astype(BF16)
        kd_ref[rows, :] = jnp.exp2(b_last - c_dec).astype(BF16)
        eb_ref[c] = jnp.exp2(b_last)


def _hgrn_intra(q, c_dec, b):
    lane = lax.broadcasted_iota(jnp.int32, (HALF, CHUNK), 1)
    srow = lax.broadcasted_iota(jnp.int32, (CHUNK, HGRN_DK), 0)
    a_rows = []
    for blk in range(CHUNK // SUB):
        r0 = blk * SUB
        halves = []
        for hf in range(2):
            t0 = r0 + hf * HALF
            q_t = q[t0:t0 + HALF, :]
            b_t = b[t0:t0 + HALF, :]
            a_half = jnp.zeros((HALF, CHUNK), F32)
            for sl in range((hf + 1) * HALF):
                s = r0 + sl
                col = jnp.sum(q_t * jnp.exp2(b_t - c_dec[s:s + 1, :]), axis=-1, keepdims=True)
                a_half = jnp.where(lane == s, col, a_half)
            halves.append(a_half)
        a_blk = jnp.concatenate(halves, axis=0)
        if blk > 0:
            ref_b = b[r0 - 1:r0, :]
            q_f = (q[r0:r0 + SUB, :] * jnp.exp2(b[r0:r0 + SUB, :] - ref_b)).astype(BF16)
            k_f = jnp.where(srow < r0, jnp.exp2(ref_b - c_dec), 0.0).astype(BF16)
            a_blk = a_blk + _dot_nt(q_f, k_f)
        a_rows.append(a_blk)
    t_idx = lax.broadcasted_iota(jnp.int32, (CHUNK, CHUNK), 0)
    s_idx = lax.broadcasted_iota(jnp.int32, (CHUNK, CHUNK), 1)
    return jnp.where(s_idx <= t_idx, jnp.concatenate(a_rows, axis=0), 0.0)


def _hgrn_kernel(xn_ref, x0_ref, w0_ref, w1_ref, w2_ref, w3_ref, lb_ref, on_ref, o_ref,
                 z_ref, st_ref, qs_ref, cs_ref, bs_ref, qi_ref, kd_ref, eb_ref, a_ref, *, heads, chunks, nt):
    step = pl.program_id(0)
    w_refs = (w0_ref, w1_ref, w2_ref, w3_ref)
    width = heads * HGRN_DK

    @pl.when(step == 0)
    def _():
        x0 = x0_ref[...]
        for part, w_ref in enumerate(w_refs):
            z_ref[0, :, part * width:(part + 1) * width] = _dot(x0, w_ref[...])

    @pl.when(step % nt == 0)
    def _():
        st_ref[...] = jnp.zeros_like(st_ref)

    r = lax.broadcasted_iota(jnp.int32, (CHUNK, CHUNK), 0)
    c = lax.broadcasted_iota(jnp.int32, (CHUNK, CHUNK), 1)
    tri = (c <= r).astype(BF16)

    def build_a(ci):
        rows = slice(ci * CHUNK, (ci + 1) * CHUNK)
        for hd in range(heads):
            cols = slice(hd * HGRN_DK, (hd + 1) * HGRN_DK)
            a_ref[ci % 2, hd] = _hgrn_intra(qs_ref[rows, cols], cs_ref[rows, cols], bs_ref[rows, cols]).astype(BF16)

    def finish(ci, slot):
        rows = slice(ci * CHUNK, (ci + 1) * CHUNK)
        for hd in range(heads):
            cols = slice(hd * HGRN_DK, (hd + 1) * HGRN_DK)
            v16 = z_ref[slot, rows, 2 * width + hd * HGRN_DK:2 * width + (hd + 1) * HGRN_DK].astype(BF16)
            st = st_ref[hd]
            o = _dot_nt(qi_ref[rows, cols], st.astype(BF16)) + _dot(a_ref[ci % 2, hd], v16)
            st_ref[hd] = st * eb_ref[ci, :, cols] + _dot_tn(v16, kd_ref[rows, cols])
            o = o * lax.rsqrt(jnp.mean(o * o, axis=-1, keepdims=True) + RMS_EPS) * on_ref[:, cols]
            gate = z_ref[slot, rows, 3 * width + hd * HGRN_DK:3 * width + (hd + 1) * HGRN_DK]
            o_ref[rows, cols] = (o * (gate * _sigmoid(gate))).astype(o_ref.dtype)

    def run(slot):
        def project_next(part):
            z_ref[1 - slot, :, part * width:(part + 1) * width] = _dot(xn_ref[...], w_refs[part][...])

        project_next(0)
        _hgrn_prep(z_ref, slot, lb_ref[...], tri, qs_ref, cs_ref, bs_ref, qi_ref, kd_ref, eb_ref, chunks=chunks)
        build_a(0)
        rest = len(w_refs) - 1
        starts = {(i * chunks + rest // 2) // rest: i + 1 for i in range(rest)}
        for ci in range(chunks):
            if ci in starts:
                project_next(starts[ci])
            finish(ci, slot)
            if ci + 1 < chunks:
                build_a(ci + 1)

    for cur in range(2):
        @pl.when(step % 2 == cur)
        def _(cur=cur):
            run(cur)


def _hgrn_mixer_core(xn, w_in, lb, o_gain, *, batch, seq, heads, tb, casts=()):
    m, k = xn.shape
    fdim = HGRN_HEADS * HGRN_DK
    tb = min(tb, seq)
    width = heads * HGRN_DK
    groups = HGRN_HEADS // heads
    nt = seq // tb
    chunks = tb // CHUNK
    n_parts = 4
    assert seq % tb == 0 and tb % CHUNK == 0 and HGRN_HEADS % heads == 0 and chunks % n_parts == 0 and nt >= 2
    steps = batch * groups * nt

    def rows_of(s):
        return (s // (nt * groups)) * nt + s % nt

    def group_of(s):
        return (s // nt) % groups

    def nxt(s):
        return jnp.minimum(s + 1, steps - 1)

    def wspec(part):
        return pl.BlockSpec((k, width), lambda s: (0, part * groups + group_of(nxt(s))))

    return _pallas(
        functools.partial(_hgrn_kernel, heads=heads, chunks=chunks, nt=nt),
        (xn, xn, w_in, w_in, w_in, w_in, lb.reshape(1, fdim), o_gain.reshape(1, fdim)),
        grid=(steps,),
        in_specs=[pl.BlockSpec((tb, k), lambda s: (rows_of(nxt(s)), 0)),
                  pl.BlockSpec((tb, k), lambda s: (0, 0), pipeline_mode=pl.Buffered(1)),
                  wspec(0), wspec(1), wspec(2), wspec(3),
                  pl.BlockSpec((1, width), lambda s: (0, group_of(s))),
                  pl.BlockSpec((1, width), lambda s: (0, group_of(s)))],
        out_specs=pl.BlockSpec((tb, width), lambda s: (rows_of(s), group_of(s))),
        out_shape=jax.ShapeDtypeStruct((m, fdim), BF16),
        scratch_shapes=[
            pltpu.VMEM((2, tb, n_parts * width), F32),
            pltpu.VMEM((heads, HGRN_DV, HGRN_DK), F32),
            pltpu.VMEM((tb, width), F32),
            pltpu.VMEM((tb, width), F32),
            pltpu.VMEM((tb, width), F32),
            pltpu.VMEM((tb, width), BF16),
            pltpu.VMEM((tb, width), BF16),
            pltpu.VMEM((chunks, 1, width), F32),
            pltpu.VMEM((2, heads, CHUNK, CHUNK), BF16),
        ],
        sem=("arbitrary",),
        name="hgrn_mixer",
        casts=casts,
    )


def _with_swapped_rope(t):
    half = QK_ROPE // 2
    return jnp.concatenate([t, t[..., -half:], t[..., -QK_ROPE:-half]], axis=-1)


def _mla_layer(h, tables, layer, mix_gain, w_a_pad, q_norm, w_uq_pad, kv_norm, w_ukv, gain_q, gain_k, w_o,
               *, batch, seq, casts=()):
    gq_pad = _with_swapped_rope(gain_q * (LOG2_E / math.sqrt(QK_HEAD))).reshape(1, HEAD_PAD)
    gk_pad = _with_swapped_rope(gain_k).reshape(1, HEAD_PAD)
    (q, k, v), (w_o16,) = _mla_proj(h, mix_gain, w_a_pad, q_norm, w_uq_pad, kv_norm, w_ukv, layer, gq_pad, gk_pad,
                                    tables, tm=256, casts=[(w_o, layer)])
    o, copies = _attention(q, k, v, batch=batch, seq=seq, blk=256, casts=casts)
    h, _ = _matmul(o, w_o16, None, residual=h, out_dtype=F32, tm=1024, tn=2048, name="mla_out")
    return h, copies


def _hgrn_layer(h, layer, mix_gain, lb, w_in, o_gain, w_o, *, batch, seq, casts=()):
    xn = _rmsnorm(h, mix_gain, tm=1024)
    o, copies = _hgrn_mixer_core(xn, w_in, lb, o_gain, batch=batch, seq=seq, heads=4, tb=512,
                                 casts=[(w_o, layer), *casts])
    h, _ = _matmul(o, copies[0], None, residual=h, out_dtype=F32, tm=1024, tn=2048, name="hgrn_out")
    return h, copies[1:]


def kernel(x, p, positions, mix_norm, ffn_norm, ple_norm, mla_w_a, mla_q_norm, mla_w_uq, mla_kv_norm, mla_w_ukv, mla_qk_gain_q, mla_qk_gain_k, mla_w_o, hgrn_lb_logits, hgrn_w_in, hgrn_o_norm, hgrn_w_o, ffn_w_in, ffn_conv_w, ffn_conv_b, ffn_w_down, ple_w_proj, ple_w_gate):
    batch, seq, d_model = x.shape
    depth = p.shape[0]
    m = batch * seq
    tables = _rope_tables(positions, tm=1024)
    sm = jax.nn.softmax(hgrn_lb_logits.astype(F32), axis=0)
    lower_bounds = jnp.cumsum(sm, axis=0) - sm[0]

    n_mla = mla_w_a.shape[0]
    w_a_pad = _with_swapped_rope(mla_w_a).astype(BF16)
    w_uq_pad = _with_swapped_rope(mla_w_uq.reshape(n_mla, Q_LORA, MLA_HEADS, QK_HEAD))
    w_uq_pad = w_uq_pad.reshape(n_mla, Q_LORA, MLA_HEADS * HEAD_PAD).astype(BF16)
    w_ukv = mla_w_ukv.astype(BF16)
    w_ple_proj = ple_w_proj.astype(BF16)
    p_rows = p.reshape(depth, m, -1)

    h = x.reshape(m, d_model)
    w_hgrn_in = None
    for i in range(depth):
        j = i // 2
        if i % 2 == 0:
            h, (w_ffn_in,) = _mla_layer(h, tables, j, mix_norm[i], w_a_pad, mla_q_norm[j], w_uq_pad, mla_kv_norm[j],
                                        w_ukv, mla_qk_gain_q[j], mla_qk_gain_k[j], mla_w_o, batch=batch, seq=seq,
                                        casts=[(ffn_w_in, i)])
        else:
            h, (w_ffn_in,) = _hgrn_layer(h, j, mix_norm[i], lower_bounds[i], w_hgrn_in, hgrn_o_norm[j], hgrn_w_o,
                                         batch=batch, seq=seq, casts=[(ffn_w_in, i)])
        act, (w_ffn_down,) = _ffn_in(h, ffn_norm[i], w_ffn_in, None, ffn_conv_w[i], ffn_conv_b[i],
                                     seq=seq, tm=1024, tn=512, casts=[(ffn_w_down, i)])
        down_casts = [(ple_w_gate, i)]
        if i + 1 < depth and (i + 1) % 2 == 1:
            down_casts.append((hgrn_w_in, (i + 1) // 2))
        h, copies = _matmul(act, w_ffn_down, None, residual=h, out_dtype=F32, tm=1024, tn=512, name="ffn_down",
                            casts=down_casts)
        w_hgrn_in = copies[1] if len(copies) > 1 else None
        h = _ple(h, ple_norm[i], copies[0], p_rows, w_ple_proj, i, tm=1024)
    return h.reshape(batch, seq, d_model)
```

```python
import functools
import math

import jax
import jax.numpy as jnp
from jax import lax
from jax.experimental import pallas as pl
from jax.experimental.pallas import tpu as pltpu

F32 = jnp.float32
BF16 = jnp.bfloat16

RMS_EPS = 1e-6
ROPE_THETA = 10000.0
LOG2_E = math.log2(math.e)

LANES = 128
BF16_SUBLANES = 16
VMEM_LIMIT_BYTES = 56 * 1024 * 1024

MLA_HEADS = 16
Q_LORA = 512
KV_LORA = 512
QK_NOPE = 128
QK_ROPE = 64
QK_HEAD = QK_NOPE + QK_ROPE
V_HEAD = 128
HEAD_PAD = 256

HGRN_HEADS = 16
HGRN_DK = 128
HGRN_DV = 128
CHUNK = 64
SUB = 16
HALF = SUB // 2

CONV_W = 3
F32_SUBLANES = 8
CONV_HALO = F32_SUBLANES


def _params(*sem):
    return pltpu.CompilerParams(dimension_semantics=sem, vmem_limit_bytes=VMEM_LIMIT_BYTES)


def _rms(x, g):
    return x * lax.rsqrt(jnp.mean(x * x, axis=-1, keepdims=True) + RMS_EPS) * g


def _sigmoid(x):
    return 1.0 / (1.0 + jnp.exp(-x))


def _dot(a, b):
    return jnp.dot(a, b, preferred_element_type=F32)


def _dot_nt(a, b):
    return lax.dot_general(a, b, (((1,), (1,)), ((), ())), preferred_element_type=F32)


def _dot_tn(a, b):
    return lax.dot_general(a, b, (((0,), (0,)), ((), ())), preferred_element_type=F32)


def _wspec(k, tn, layer, col, resident=False):
    mode = dict(pipeline_mode=pl.Buffered(1)) if resident else {}
    if layer is None:
        return pl.BlockSpec((k, tn), lambda *g: (0, col(*g)), **mode)
    return pl.BlockSpec((None, k, tn), lambda *g: (layer, 0, col(*g)), **mode)


def _pallas(body, ins, *, grid, in_specs, out_specs, out_shape, sem, name, scratch_shapes=(), casts=()):
    single = not isinstance(out_shape, (list, tuple))
    out_specs = [out_specs] if single else list(out_specs)
    out_shape = [out_shape] if single else list(out_shape)
    n_in, n_out, n_cast = len(ins), len(out_shape), len(casts)
    steps = math.prod(grid)

    def step_of(*g):
        idx = 0
        for size, pos in zip(grid, g):
            idx = idx * size + pos
        return idx

    c_in, c_in_specs, c_out_specs, c_out_shape, c_shapes = [], [], [], [], []
    for stack, layer in casts:
        layers, rows, cols = stack.shape
        size = rows * cols
        view_cols = next(c for c in (cols, cols // 2, cols // 4, cols // 8)
                         if c % LANES == 0 and (size // c) % (steps * BF16_SUBLANES) == 0)
        view_rows = size // view_cols
        rb = view_rows // steps
        c_in.append(stack.reshape(layers, view_rows, view_cols))
        c_in_specs.append(pl.BlockSpec((None, rb, view_cols), lambda *g, layer=layer: (layer, step_of(*g), 0)))
        c_out_specs.append(pl.BlockSpec((rb, view_cols), lambda *g: (step_of(*g), 0)))
        c_out_shape.append(jax.ShapeDtypeStruct((view_rows, view_cols), BF16))
        c_shapes.append((rows, cols))

    def kernel(*refs):
        outs_at = n_in + n_cast
        copies_at = outs_at + n_out
        for src, dst in zip(refs[n_in:outs_at], refs[copies_at:copies_at + n_cast]):
            dst[...] = src[...].astype(BF16)
        body(*refs[:n_in], *refs[outs_at:copies_at], *refs[copies_at + n_cast:])

    res = pl.pallas_call(
        kernel,
        grid=grid,
        in_specs=[*in_specs, *c_in_specs],
        out_specs=[*out_specs, *c_out_specs],
        out_shape=[*out_shape, *c_out_shape],
        scratch_shapes=list(scratch_shapes),
        compiler_params=_params(*sem),
        name=name,
    )(*ins, *c_in)
    outs = res[0] if single else list(res[:n_out])
    return outs, [copy.reshape(shape) for copy, shape in zip(res[n_out:], c_shapes)]


def _matmul_kernel(*refs, norm, residual):
    it = iter(refs)
    x_ref = next(it)
    g_ref = next(it) if norm else None
    w_ref = next(it)
    r_ref = next(it) if residual else None
    o_ref = next(it)
    xn_ref = next(it) if norm else None

    if norm:
        @pl.when(pl.program_id(1) == 0)
        def _():
            xn_ref[...] = _rms(x_ref[...], g_ref[...]).astype(BF16)
        acc = _dot(xn_ref[...], w_ref[...])
    else:
        acc = _dot(x_ref[...], w_ref[...])
    if residual:
        acc = r_ref[...] + acc
    o_ref[...] = acc.astype(o_ref.dtype)


def _matmul(x, w, layer, *, gain=None, residual=None, out_dtype, tm, tn, name, casts=()):
    m, k = x.shape
    n = w.shape[-1]
    tm, tn = min(tm, m), min(tn, n)
    assert m % tm == 0 and n % tn == 0
    norm = gain is not None
    ins = [x]
    specs = [pl.BlockSpec((tm, k), lambda i, j: (i, 0))]
    if norm:
        ins.append(gain.reshape(1, k))
        specs.append(pl.BlockSpec((1, k), lambda i, j: (0, 0)))
    ins.append(w)
    specs.append(_wspec(k, tn, layer, lambda i, j: j, resident=tn == n))
    if residual is not None:
        ins.append(residual)
        specs.append(pl.BlockSpec((tm, tn), lambda i, j: (i, j)))
    return _pallas(
        functools.partial(_matmul_kernel, norm=norm, residual=residual is not None), ins,
        grid=(m // tm, n // tn),
        in_specs=specs,
        out_specs=pl.BlockSpec((tm, tn), lambda i, j: (i, j)),
        out_shape=jax.ShapeDtypeStruct((m, n), out_dtype),
        scratch_shapes=[pltpu.VMEM((tm, k), BF16)] if norm else [],
        sem=("parallel", "arbitrary"),
        name=name,
        casts=casts,
    )


def _ffn_in_kernel(x_ref, g_ref, wg_ref, wu_ref, cw_ref, cb_ref, o_ref, xn_ref, gate_ref, carry_ref,
                   *, tm, tiles_per_seq):
    i = pl.program_id(0)
    j = pl.program_id(1)

    @pl.when(j == 0)
    def _():
        xn_ref[...] = _rms(x_ref[...], g_ref[...]).astype(BF16)

    seq_start = i % tiles_per_seq == 0

    @pl.when(seq_start)
    def _():
        gate_ref[:CONV_HALO, :] = jnp.zeros((CONV_HALO, gate_ref.shape[1]), F32)

    @pl.when(jnp.logical_not(seq_start))
    def _():
        gate_ref[:CONV_HALO, :] = carry_ref[j]

    gate_ref[CONV_HALO:, :] = _dot(xn_ref[...], wg_ref[...])
    up = _dot(xn_ref[...], wu_ref[...])
    carry_ref[j] = gate_ref[tm:, :]
    cw = cw_ref[...]
    conv = cb_ref[...] + gate_ref[pl.ds(CONV_HALO - 2, tm), :] * cw[0:1, :]
    conv = conv + gate_ref[pl.ds(CONV_HALO - 1, tm), :] * cw[1:2, :]
    conv = conv + gate_ref[pl.ds(CONV_HALO, tm), :] * cw[2:3, :]
    o_ref[...] = (conv * _sigmoid(conv) * up).astype(o_ref.dtype)


def _ffn_in(h, gain, w_in, layer, conv_w, conv_b, *, seq, tm, tn, casts=()):
    m, k = h.shape
    d_ff = w_in.shape[-1] // 2
    tm = min(tm, seq)
    assert seq % tm == 0 and d_ff % tn == 0 and tm % CONV_HALO == 0
    nj = d_ff // tn
    return _pallas(
        functools.partial(_ffn_in_kernel, tm=tm, tiles_per_seq=seq // tm),
        (h, gain.reshape(1, k), w_in, w_in, conv_w, conv_b.reshape(1, d_ff)),
        grid=(m // tm, nj),
        in_specs=[
            pl.BlockSpec((tm, k), lambda i, j: (i, 0)),
            pl.BlockSpec((1, k), lambda i, j: (0, 0)),
            _wspec(k, tn, layer, lambda i, j: j),
            _wspec(k, tn, layer, lambda i, j: j + nj),
            pl.BlockSpec((CONV_W, tn), lambda i, j: (0, j)),
            pl.BlockSpec((1, tn), lambda i, j: (0, j)),
        ],
        out_specs=pl.BlockSpec((tm, tn), lambda i, j: (i, j)),
        out_shape=jax.ShapeDtypeStruct((m, d_ff), BF16),
        scratch_shapes=[pltpu.VMEM((tm, k), BF16), pltpu.VMEM((tm + CONV_HALO, tn), F32),
                        pltpu.VMEM((nj, CONV_HALO, tn), F32)],
        sem=("arbitrary", "arbitrary"),
        name="ffn_in",
        casts=casts,
    )


def _ple_kernel(*refs, with_next):
    if with_next:
        h_ref, g_ref, wg_ref, p_ref, wp_ref, gn_ref, o_ref, xn_ref = refs
    else:
        h_ref, g_ref, wg_ref, p_ref, wp_ref, o_ref = refs
    h = h_ref[...]
    proj = _dot(p_ref[...].astype(BF16), wp_ref[...])
    gate = _dot(_rms(h, g_ref[...]).astype(BF16), wg_ref[...])
    out = h + proj * _sigmoid(gate)
    o_ref[...] = out
    if with_next:
        xn_ref[...] = _rms(out, gn_ref[...]).astype(BF16)


def _ple(h, gain, w_gate, p, w_proj, layer, *, tm, next_gain=None):
    m, k = h.shape
    n = w_gate.shape[-1]
    kp = p.shape[2]
    tm = min(tm, m)
    assert m % tm == 0 and n == k
    gate_layer = layer if w_gate.ndim == 3 else None
    with_next = next_gain is not None
    row = pl.BlockSpec((tm, n), lambda i: (i, 0))
    ins = [h, gain.reshape(1, k), w_gate, p, w_proj]
    specs = [
        pl.BlockSpec((tm, k), lambda i: (i, 0)),
        pl.BlockSpec((1, k), lambda i: (0, 0)),
        _wspec(k, n, gate_layer, lambda i: 0, resident=True),
        pl.BlockSpec((None, tm, kp), lambda i: (layer, i, 0)),
        _wspec(kp, n, layer, lambda i: 0, resident=True),
    ]
    if with_next:
        ins.append(next_gain.reshape(1, n))
        specs.append(pl.BlockSpec((1, n), lambda i: (0, 0)))
    return pl.pallas_call(
        functools.partial(_ple_kernel, with_next=with_next),
        grid=(m // tm,),
        in_specs=specs,
        out_specs=[row, row] if with_next else row,
        out_shape=([jax.ShapeDtypeStruct((m, n), F32), jax.ShapeDtypeStruct((m, n), BF16)] if with_next
                   else jax.ShapeDtypeStruct((m, n), F32)),
        compiler_params=_params("parallel"),
        name="ple",
    )(*ins)


def _rope_table_kernel(pos_ref, inv_ref, c_ref, s_ref):
    ang = pos_ref[...].astype(F32) * inv_ref[...]
    lane = lax.broadcasted_iota(jnp.int32, ang.shape, 1)
    c = jnp.cos(ang)
    s = jnp.sin(ang)
    half = QK_ROPE // 2
    c_ref[...] = jnp.where(lane < QK_ROPE, c, 0.0)
    s_ref[...] = jnp.where(lane < half, -s, jnp.where(lane < QK_ROPE, s, 0.0))


def _rope_tables(positions, *, tm):
    m = positions.size
    tm = min(tm, m)
    half = QK_ROPE // 2
    inv_freq = ROPE_THETA ** (-jnp.arange(0, QK_ROPE, 2, dtype=F32) / QK_ROPE)
    inv_row = jnp.concatenate([inv_freq, inv_freq, jnp.zeros((LANES - 2 * half,), F32)]).reshape(1, LANES)
    out = jax.ShapeDtypeStruct((m, LANES), F32)
    spec = pl.BlockSpec((tm, LANES), lambda i: (i, 0))
    return pl.pallas_call(
        _rope_table_kernel,
        grid=(m // tm,),
        in_specs=[pl.BlockSpec((tm, 1), lambda i: (i, 0)), pl.BlockSpec((1, LANES), lambda i: (0, 0))],
        out_specs=[spec, spec],
        out_shape=[out, out],
        compiler_params=_params("parallel"),
        name="rope_tables",
    )(positions.reshape(m, 1), inv_row)


def _mla_proj_kernel(h_ref, h0_ref, g_ref, wa_ref, qn_ref, kvn_ref, wuq_ref, wukv_ref, gq_ref, gk_ref,
                     c_ref, s_ref, q_out, k_out, v_out, qraw_ref, kvraw_ref, kpe_ref):
    step = pl.program_id(0)

    def project(x, dst):
        xn = _rms(x, g_ref[...]).astype(BF16)
        c = _dot(xn, wa_ref[...])
        cq = _rms(c[:, :Q_LORA], qn_ref[...]).astype(BF16)
        ckv = _rms(c[:, Q_LORA:Q_LORA + KV_LORA], kvn_ref[...]).astype(BF16)
        kpe_ref[dst] = c[:, Q_LORA + KV_LORA:]
        qraw_ref[dst] = _dot(cq, wuq_ref[...])
        kvraw_ref[dst] = _dot(ckv, wukv_ref[...])

    def finish(slot):
        k_pe = kpe_ref[slot]
        cos_t, sin_t = c_ref[...], s_ref[...]

        def rope(r):
            return r * cos_t + pltpu.roll(r, LANES // 2, 1) * sin_t

        def row_total(sq):
            return jnp.sum(sq, axis=-1, keepdims=True)

        gq = gq_ref[...]
        gk = gk_ref[...]
        k_pe_sq = row_total(0.5 * (k_pe * k_pe))
        k_rope = rope(k_pe * gk[:, QK_NOPE:])

        for hd in range(MLA_HEADS):
            lo = hd * HEAD_PAD
            mid = lo + QK_NOPE
            hi = lo + HEAD_PAD
            q_nope, q_rope = qraw_ref[slot, :, lo:mid], qraw_ref[slot, :, mid:hi]
            q_sq = row_total(q_nope * q_nope + 0.5 * (q_rope * q_rope))
            inv = lax.rsqrt(q_sq * (1.0 / QK_HEAD) + RMS_EPS)
            q_out[:, lo:mid] = (q_nope * inv * gq[:, :QK_NOPE]).astype(BF16)
            q_out[:, mid:hi] = rope(q_rope * inv * gq[:, QK_NOPE:]).astype(BF16)

            kn = kvraw_ref[slot, :, lo:mid]
            inv_k = lax.rsqrt((row_total(kn * kn) + k_pe_sq) * (1.0 / QK_HEAD) + RMS_EPS)
            k_out[:, lo:mid] = (kn * inv_k * gk[:, :QK_NOPE]).astype(BF16)
            k_out[:, mid:hi] = (k_rope * inv_k).astype(BF16)
            v_out[:, hd * V_HEAD:(hd + 1) * V_HEAD] = kvraw_ref[slot, :, mid:hi].astype(BF16)

    @pl.when(step == 0)
    def _():
        project(h0_ref[...], 0)

    for cur in range(2):
        @pl.when(step % 2 == cur)
        def _(cur=cur):
            project(h_ref[...], 1 - cur)
            finish(cur)


def _mla_proj(h, gain, w_a, q_norm, w_uq_pad, kv_norm, w_ukv, layer, gq_pad, gk_pad, tables, *, tm, casts=()):
    m, k = h.shape
    tm = min(tm, m)
    assert m % tm == 0
    a_dim = w_a.shape[2]
    nq = MLA_HEADS * HEAD_PAD
    steps = m // tm
    const = lambda i: (0, 0)
    row = lambda i: (i, 0)
    return _pallas(
        _mla_proj_kernel,
        (h, h, gain.reshape(1, k), w_a, q_norm.reshape(1, Q_LORA), kv_norm.reshape(1, KV_LORA),
         w_uq_pad, w_ukv, gq_pad, gk_pad, *tables),
        grid=(steps,),
        in_specs=[
            pl.BlockSpec((tm, k), lambda i: (jnp.minimum(i + 1, steps - 1), 0)),
            pl.BlockSpec((tm, k), const, pipeline_mode=pl.Buffered(1)),
            pl.BlockSpec((1, k), const),
            _wspec(k, a_dim, layer, lambda i: 0, resident=True),
            pl.BlockSpec((1, Q_LORA), const),
            pl.BlockSpec((1, KV_LORA), const),
            _wspec(Q_LORA, nq, layer, lambda i: 0, resident=True),
            _wspec(KV_LORA, nq, layer, lambda i: 0, resident=True),
            pl.BlockSpec((1, HEAD_PAD), const),
            pl.BlockSpec((1, HEAD_PAD), const),
            pl.BlockSpec((tm, LANES), row),
            pl.BlockSpec((tm, LANES), row),
        ],
        out_specs=[pl.BlockSpec((tm, nq), row), pl.BlockSpec((tm, nq), row),
                   pl.BlockSpec((tm, MLA_HEADS * V_HEAD), row)],
        out_shape=[jax.ShapeDtypeStruct((m, nq), BF16), jax.ShapeDtypeStruct((m, nq), BF16),
                   jax.ShapeDtypeStruct((m, MLA_HEADS * V_HEAD), BF16)],
        scratch_shapes=[pltpu.VMEM((2, tm, nq), F32),
                        pltpu.VMEM((2, tm, nq), F32),
                        pltpu.VMEM((2, tm, LANES), F32)],
        sem=("arbitrary",),
        name="mla_proj",
        casts=casts,
    )


def _attn_kernel(q_ref, k_ref, v_ref, o_ref, *, blk):
    seq = q_ref.shape[0]
    row = lax.broadcasted_iota(jnp.int32, (blk, blk), 0)
    col = lax.broadcasted_iota(jnp.int32, (blk, blk), 1)
    causal = col <= row
    for qi in range(seq // blk):
        q = q_ref[qi * blk:(qi + 1) * blk, :]
        m_run = jnp.full((blk, 1), -jnp.inf, F32)
        l_run = jnp.zeros((blk, 1), F32)
        acc = jnp.zeros((blk, V_HEAD), F32)
        for ki in range(qi + 1):
            s = _dot_nt(q, k_ref[ki * blk:(ki + 1) * blk, :])
            if ki == qi:
                s = jnp.where(causal, s, -jnp.inf)
            m_new = jnp.maximum(m_run, jnp.max(s, axis=-1, keepdims=True))
            p = jnp.exp2(s - m_new)
            alpha = jnp.exp2(m_run - m_new)
            l_run = alpha * l_run + jnp.sum(p, axis=-1, keepdims=True)
            acc = alpha * acc + _dot(p.astype(BF16), v_ref[ki * blk:(ki + 1) * blk, :])
            m_run = m_new
        o_ref[qi * blk:(qi + 1) * blk, :] = (acc / l_run).astype(o_ref.dtype)


def _attention(q, k, v, *, batch, seq, blk, casts=()):
    m = q.shape[0]
    blk = min(blk, seq)
    assert seq % blk == 0
    return _pallas(
        functools.partial(_attn_kernel, blk=blk), (q, k, v),
        grid=(batch, MLA_HEADS),
        in_specs=[
            pl.BlockSpec((seq, HEAD_PAD), lambda b, h: (b, h)),
            pl.BlockSpec((seq, HEAD_PAD), lambda b, h: (b, h)),
            pl.BlockSpec((seq, V_HEAD), lambda b, h: (b, h)),
        ],
        out_specs=pl.BlockSpec((seq, V_HEAD), lambda b, h: (b, h)),
        out_shape=jax.ShapeDtypeStruct((m, MLA_HEADS * V_HEAD), BF16),
        sem=("parallel", "parallel"),
        name="mla_attention",
        casts=casts,
    )


def _hgrn_prep(z_ref, slot, lb, tri, qs_ref, cs_ref, bs_ref, qi_ref, kd_ref, eb_ref, *, chunks):
    width = lb.shape[1]
    for c in range(chunks):
        rows = slice(c * CHUNK, (c + 1) * CHUNK)
        ff = z_ref[slot, rows, width:2 * width]
        e_abs = jnp.exp(-jnp.abs(ff))
        r_abs = 1.0 / (1.0 + e_abs)
        f = lb + (1.0 - lb) * jnp.where(ff >= 0.0, r_abs, e_abs * r_abs)
        log_f = jnp.log2(f)
        log_k = jnp.log2(1.0 - lb) - (jnp.maximum(ff, 0.0) * LOG2_E + jnp.log2(1.0 + e_abs))
        q_raw = z_ref[slot, rows, 0:width]
        q = q_raw * _sigmoid(q_raw)
        hi = log_f.astype(BF16)
        r1 = log_f - hi.astype(F32)
        mid = r1.astype(BF16)
        lo = (r1 - mid.astype(F32)).astype(BF16)
        parts = _dot(tri, jnp.concatenate([hi, mid, lo], axis=1))
        b = parts[:, :width] + parts[:, width:2 * width] + parts[:, 2 * width:]
        b_last = b[CHUNK - 1:CHUNK, :]
        c_dec = b - log_k
        qs_ref[rows, :] = q
        cs_ref[rows, :] = c_dec
        bs_ref[rows, :] = b
        qi_ref[rows, :] = (q * jnp.exp2(b)).astype(BF16)
        kd_ref[rows, :] = jnp.exp2(b_last - c_dec).astype(BF16)
        eb_ref[c] = jnp.exp2(b_last)


def _hgrn_intra(q, c_dec, b):
    lane = lax.broadcasted_iota(jnp.int32, (HALF, CHUNK), 1)
    srow = lax.broadcasted_iota(jnp.int32, (CHUNK, HGRN_DK), 0)
    a_rows = []
    for blk in range(CHUNK // SUB):
        r0 = blk * SUB
        halves = []
        for hf in range(2):
            t0 = r0 + hf * HALF
            q_t = q[t0:t0 + HALF, :]
            b_t = b[t0:t0 + HALF, :]
            a_half = jnp.zeros((HALF, CHUNK), F32)
            for sl in range((hf + 1) * HALF):
                s = r0 + sl
                col = jnp.sum(q_t * jnp.exp2(b_t - c_dec[s:s + 1, :]), axis=-1, keepdims=True)
                a_half = jnp.where(lane == s, col, a_half)
            halves.append(a_half)
        a_blk = jnp.concatenate(halves, axis=0)
        if blk > 0:
            ref_b = b[r0 - 1:r0, :]
            q_f = (q[r0:r0 + SUB, :] * jnp.exp2(b[r0:r0 + SUB, :] - ref_b)).astype(BF16)
            k_f = jnp.where(srow < r0, jnp.exp2(ref_b - c_dec), 0.0).astype(BF16)
            a_blk = a_blk + _dot_nt(q_f, k_f)
        a_rows.append(a_blk)
    t_idx = lax.broadcasted_iota(jnp.int32, (CHUNK, CHUNK), 0)
    s_idx = lax.broadcasted_iota(jnp.int32, (CHUNK, CHUNK), 1)
    return jnp.where(s_idx <= t_idx, jnp.concatenate(a_rows, axis=0), 0.0)


def _hgrn_kernel(xn_ref, x0_ref, w0_ref, w1_ref, w2_ref, w3_ref, lb_ref, on_ref, o_ref,
                 z_ref, st_ref, qs_ref, cs_ref, bs_ref, qi_ref, kd_ref, eb_ref, a_ref, *, heads, chunks, nt):
    step = pl.program_id(0)
    w_refs = (w0_ref, w1_ref, w2_ref, w3_ref)
    width = heads * HGRN_DK

    @pl.when(step == 0)
    def _():
        x0 = x0_ref[...]
        for part, w_ref in enumerate(w_refs):
            z_ref[0, :, part * width:(part + 1) * width] = _dot(x0, w_ref[...])

    @pl.when(step % nt == 0)
    def _():
        st_ref[...] = jnp.zeros_like(st_ref)

    r = lax.broadcasted_iota(jnp.int32, (CHUNK, CHUNK), 0)
    c = lax.broadcasted_iota(jnp.int32, (CHUNK, CHUNK), 1)
    tri = (c <= r).astype(BF16)

    def build_a(ci):
        rows = slice(ci * CHUNK, (ci + 1) * CHUNK)
        for hd in range(heads):
            cols = slice(hd * HGRN_DK, (hd + 1) * HGRN_DK)
            a_ref[ci % 2, hd] = _hgrn_intra(qs_ref[rows, cols], cs_ref[rows, cols], bs_ref[rows, cols]).astype(BF16)

    def finish(ci, slot):
        rows = slice(ci * CHUNK, (ci + 1) * CHUNK)
        for hd in range(heads):
            cols = slice(hd * HGRN_DK, (hd + 1) * HGRN_DK)
            v16 = z_ref[slot, rows, 2 * width + hd * HGRN_DK:2 * width + (hd + 1) * HGRN_DK].astype(BF16)
            st = st_ref[hd]
            o = _dot_nt(qi_ref[rows, cols], st.astype(BF16)) + _dot(a_ref[ci % 2, hd], v16)
            st_ref[hd] = st * eb_ref[ci, :, cols] + _dot_tn(v16, kd_ref[rows, cols])
            o = o * lax.rsqrt(jnp.mean(o * o, axis=-1, keepdims=True) + RMS_EPS) * on_ref[:, cols]
            gate = z_ref[slot, rows, 3 * width + hd * HGRN_DK:3 * width + (hd + 1) * HGRN_DK]
            o_ref[rows, cols] = (o * (gate * _sigmoid(gate))).astype(o_ref.dtype)

    def run(slot):
        def project_next(part):
            z_ref[1 - slot, :, part * width:(part + 1) * width] = _dot(xn_ref[...], w_refs[part][...])

        project_next(0)
        _hgrn_prep(z_ref, slot, lb_ref[...], tri, qs_ref, cs_ref, bs_ref, qi_ref, kd_ref, eb_ref, chunks=chunks)
        build_a(0)
        rest = len(w_refs) - 1
        starts = {(i * chunks + rest // 2) // rest: i + 1 for i in range(rest)}
        for ci in range(chunks):
            if ci in starts:
                project_next(starts[ci])
            finish(ci, slot)
            if ci + 1 < chunks:
                build_a(ci + 1)

    for cur in range(2):
        @pl.when(step % 2 == cur)
        def _(cur=cur):
            run(cur)


def _hgrn_mixer_core(xn, w_in, lb, o_gain, *, batch, seq, heads, tb, casts=()):
    m, k = xn.shape
    fdim = HGRN_HEADS * HGRN_DK
    tb = min(tb, seq)
    width = heads * HGRN_DK
    groups = HGRN_HEADS // heads
    nt = seq // tb
    chunks = tb // CHUNK
    n_parts = 4
    assert seq % tb == 0 and tb % CHUNK == 0 and HGRN_HEADS % heads == 0 and chunks % n_parts == 0 and nt >= 2
    steps = batch * groups * nt

    def rows_of(s):
        return (s // (nt * groups)) * nt + s % nt

    def group_of(s):
        return (s // nt) % groups

    def nxt(s):
        return jnp.minimum(s + 1, steps - 1)

    def wspec(part):
        return pl.BlockSpec((k, width), lambda s: (0, part * groups + group_of(nxt(s))))

    return _pallas(
        functools.partial(_hgrn_kernel, heads=heads, chunks=chunks, nt=nt),
        (xn, xn, w_in, w_in, w_in, w_in, lb.reshape(1, fdim), o_gain.reshape(1, fdim)),
        grid=(steps,),
        in_specs=[pl.BlockSpec((tb, k), lambda s: (rows_of(nxt(s)), 0)),
                  pl.BlockSpec((tb, k), lambda s: (0, 0), pipeline_mode=pl.Buffered(1)),
                  wspec(0), wspec(1), wspec(2), wspec(3),
                  pl.BlockSpec((1, width), lambda s: (0, group_of(s))),
                  pl.BlockSpec((1, width), lambda s: (0, group_of(s)))],
        out_specs=pl.BlockSpec((tb, width), lambda s: (rows_of(s), group_of(s))),
        out_shape=jax.ShapeDtypeStruct((m, fdim), BF16),
        scratch_shapes=[
            pltpu.VMEM((2, tb, n_parts * width), F32),
            pltpu.VMEM((heads, HGRN_DV, HGRN_DK), F32),
            pltpu.VMEM((tb, width), F32),
            pltpu.VMEM((tb, width), F32),
            pltpu.VMEM((tb, width), F32),
            pltpu.VMEM((tb, width), BF16),
            pltpu.VMEM((tb, width), BF16),
            pltpu.VMEM((chunks, 1, width), F32),
            pltpu.VMEM((2, heads, CHUNK, CHUNK), BF16),
        ],
        sem=("arbitrary",),
        name="hgrn_mixer",
        casts=casts,
    )


def _with_swapped_rope(t):
    half = QK_ROPE // 2
    return jnp.concatenate([t, t[..., -half:], t[..., -QK_ROPE:-half]], axis=-1)


def _mla_layer(h, tables, layer, mix_gain, w_a_pad, q_norm, w_uq_pad, kv_norm, w_ukv, gain_q, gain_k, w_o,
               *, batch, seq, casts=()):
    gq_pad = _with_swapped_rope(gain_q * (LOG2_E / math.sqrt(QK_HEAD))).reshape(1, HEAD_PAD)
    gk_pad = _with_swapped_rope(gain_k).reshape(1, HEAD_PAD)
    (q, k, v), (w_o16,) = _mla_proj(h, mix_gain, w_a_pad, q_norm, w_uq_pad, kv_norm, w_ukv, layer, gq_pad, gk_pad,
                                    tables, tm=256, casts=[(w_o, layer)])
    o, copies = _attention(q, k, v, batch=batch, seq=seq, blk=256, casts=casts)
    h, _ = _matmul(o, w_o16, None, residual=h, out_dtype=F32, tm=1024, tn=2048, name="mla_out")
    return h, copies


def _hgrn_layer(h, xn, layer, lb, w_in, o_gain, w_o, *, batch, seq, casts=()):
    o, copies = _hgrn_mixer_core(xn, w_in, lb, o_gain, batch=batch, seq=seq, heads=4, tb=512,
                                 casts=[(w_o, layer), *casts])
    h, _ = _matmul(o, copies[0], None, residual=h, out_dtype=F32, tm=1024, tn=2048, name="hgrn_out")
    return h, copies[1:]


def kernel(x, p, positions, mix_norm, ffn_norm, ple_norm, mla_w_a, mla_q_norm, mla_w_uq, mla_kv_norm, mla_w_ukv, mla_qk_gain_q, mla_qk_gain_k, mla_w_o, hgrn_lb_logits, hgrn_w_in, hgrn_o_norm, hgrn_w_o, ffn_w_in, ffn_conv_w, ffn_conv_b, ffn_w_down, ple_w_proj, ple_w_gate):
    batch, seq, d_model = x.shape
    depth = p.shape[0]
    m = batch * seq
    tables = _rope_tables(positions, tm=1024)
    sm = jax.nn.softmax(hgrn_lb_logits.astype(F32), axis=0)
    lower_bounds = jnp.cumsum(sm, axis=0) - sm[0]

    n_mla = mla_w_a.shape[0]
    w_a_pad = _with_swapped_rope(mla_w_a).astype(BF16)
    w_uq_pad = _with_swapped_rope(mla_w_uq.reshape(n_mla, Q_LORA, MLA_HEADS, QK_HEAD))
    w_uq_pad = w_uq_pad.reshape(n_mla, Q_LORA, MLA_HEADS * HEAD_PAD).astype(BF16)
    w_ukv = mla_w_ukv.astype(BF16)
    w_ple_proj = ple_w_proj.astype(BF16)
    p_rows = p.reshape(depth, m, -1)

    h = x.reshape(m, d_model)
    xn = w_hgrn_in = None
    for i in range(depth):
        j = i // 2
        hgrn_next = i + 1 < depth and (i + 1) % 2 == 1
        if i % 2 == 0:
            mixer_casts = [(ffn_w_in, i)] + ([(hgrn_w_in, (i + 1) // 2)] if hgrn_next else [])
            h, copies = _mla_layer(h, tables, j, mix_norm[i], w_a_pad, mla_q_norm[j], w_uq_pad, mla_kv_norm[j],
                                   w_ukv, mla_qk_gain_q[j], mla_qk_gain_k[j], mla_w_o, batch=batch, seq=seq,
                                   casts=mixer_casts)
            w_ffn_in = copies[0]
            w_hgrn_in = copies[1] if hgrn_next else None
        else:
            h, (w_ffn_in,) = _hgrn_layer(h, xn, j, lower_bounds[i], w_hgrn_in, hgrn_o_norm[j], hgrn_w_o,
                                         batch=batch, seq=seq, casts=[(ffn_w_in, i)])
        act, (w_ffn_down,) = _ffn_in(h, ffn_norm[i], w_ffn_in, None, ffn_conv_w[i], ffn_conv_b[i],
                                     seq=seq, tm=1024, tn=512, casts=[(ffn_w_down, i)])
        h, (w_ple_gate,) = _matmul(act, w_ffn_down, None, residual=h, out_dtype=F32, tm=1024, tn=512,
                                   name="ffn_down", casts=[(ple_w_gate, i)])
        if hgrn_next:
            h, xn = _ple(h, ple_norm[i], w_ple_gate, p_rows, w_ple_proj, i, tm=512, next_gain=mix_norm[i + 1])
        else:
            h = _ple(h, ple_norm[i], w_ple_gate, p_rows, w_ple_proj, i, tm=512)
    return h.reshape(batch, seq, d_model)
```

```python
import functools
import math

import jax
import jax.numpy as jnp
from jax import lax
from jax.experimental import pallas as pl
from jax.experimental.pallas import tpu as pltpu

F32 = jnp.float32
BF16 = jnp.bfloat16

RMS_EPS = 1e-6
ROPE_THETA = 10000.0
LOG2_E = math.log2(math.e)

LANES = 128
BF16_SUBLANES = 16
VMEM_LIMIT_BYTES = 56 * 1024 * 1024

MLA_HEADS = 16
Q_LORA = 512
KV_LORA = 512
QK_NOPE = 128
QK_ROPE = 64
QK_HEAD = QK_NOPE + QK_ROPE
V_HEAD = 128
HEAD_PAD = 256

HGRN_HEADS = 16
HGRN_DK = 128
HGRN_DV = 128
CHUNK = 64
SUB = 16
HALF = SUB // 2

CONV_W = 3
F32_SUBLANES = 8
CONV_HALO = F32_SUBLANES


def _params(*sem):
    return pltpu.CompilerParams(dimension_semantics=sem, vmem_limit_bytes=VMEM_LIMIT_BYTES)


def _rms(x, g):
    return x * lax.rsqrt(jnp.mean(x * x, axis=-1, keepdims=True) + RMS_EPS) * g


def _sigmoid(x):
    return 1.0 / (1.0 + jnp.exp(-x))


def _dot(a, b):
    return jnp.dot(a, b, preferred_element_type=F32)


def _dot_nt(a, b):
    return lax.dot_general(a, b, (((1,), (1,)), ((), ())), preferred_element_type=F32)


def _dot_tn(a, b):
    return lax.dot_general(a, b, (((0,), (0,)), ((), ())), preferred_element_type=F32)


def _wspec(k, tn, layer, col, resident=False):
    mode = dict(pipeline_mode=pl.Buffered(1)) if resident else {}
    if layer is None:
        return pl.BlockSpec((k, tn), lambda *g: (0, col(*g)), **mode)
    return pl.BlockSpec((None, k, tn), lambda *g: (layer, 0, col(*g)), **mode)


def _pallas(body, ins, *, grid, in_specs, out_specs, out_shape, sem, name, scratch_shapes=(), casts=()):
    single = not isinstance(out_shape, (list, tuple))
    out_specs = [out_specs] if single else list(out_specs)
    out_shape = [out_shape] if single else list(out_shape)
    n_in, n_out, n_cast = len(ins), len(out_shape), len(casts)
    steps = math.prod(grid)

    def step_of(*g):
        idx = 0
        for size, pos in zip(grid, g):
            idx = idx * size + pos
        return idx

    c_in, c_in_specs, c_out_specs, c_out_shape, c_shapes = [], [], [], [], []
    for stack, layer in casts:
        layers, rows, cols = stack.shape
        size = rows * cols
        view_cols = next(c for c in (cols, cols // 2, cols // 4, cols // 8)
                         if c % LANES == 0 and (size // c) % (steps * BF16_SUBLANES) == 0)
        view_rows = size // view_cols
        rb = view_rows // steps
        c_in.append(stack.reshape(layers, view_rows, view_cols))
        c_in_specs.append(pl.BlockSpec((None, rb, view_cols), lambda *g, layer=layer: (layer, step_of(*g), 0)))
        c_out_specs.append(pl.BlockSpec((rb, view_cols), lambda *g: (step_of(*g), 0)))
        c_out_shape.append(jax.ShapeDtypeStruct((view_rows, view_cols), BF16))
        c_shapes.append((rows, cols))

    def kernel(*refs):
        outs_at = n_in + n_cast
        copies_at = outs_at + n_out
        for src, dst in zip(refs[n_in:outs_at], refs[copies_at:copies_at + n_cast]):
            dst[...] = src[...].astype(BF16)
        body(*refs[:n_in], *refs[outs_at:copies_at], *refs[copies_at + n_cast:])

    res = pl.pallas_call(
        kernel,
        grid=grid,
        in_specs=[*in_specs, *c_in_specs],
        out_specs=[*out_specs, *c_out_specs],
        out_shape=[*out_shape, *c_out_shape],
        scratch_shapes=list(scratch_shapes),
        compiler_params=_params(*sem),
        name=name,
    )(*ins, *c_in)
    outs = res[0] if single else list(res[:n_out])
    return outs, [copy.reshape(shape) for copy, shape in zip(res[n_out:], c_shapes)]


def _matmul_kernel(*refs, norm, residual):
    it = iter(refs)
    x_ref = next(it)
    g_ref = next(it) if norm else None
    w_ref = next(it)
    r_ref = next(it) if residual else None
    o_ref = next(it)
    xn_ref = next(it) if norm else None

    if norm:
        @pl.when(pl.program_id(1) == 0)
        def _():
            xn_ref[...] = _rms(x_ref[...], g_ref[...]).astype(BF16)
        acc = _dot(xn_ref[...], w_ref[...])
    else:
        acc = _dot(x_ref[...], w_ref[...])
    if residual:
        acc = r_ref[...] + acc
    o_ref[...] = acc.astype(o_ref.dtype)


def _matmul(x, w, layer, *, gain=None, residual=None, out_dtype, tm, tn, name, casts=()):
    m, k = x.shape
    n = w.shape[-1]
    tm, tn = min(tm, m), min(tn, n)
    assert m % tm == 0 and n % tn == 0
    norm = gain is not None
    ins = [x]
    specs = [pl.BlockSpec((tm, k), lambda i, j: (i, 0))]
    if norm:
        ins.append(gain.reshape(1, k))
        specs.append(pl.BlockSpec((1, k), lambda i, j: (0, 0)))
    ins.append(w)
    specs.append(_wspec(k, tn, layer, lambda i, j: j, resident=tn == n))
    if residual is not None:
        ins.append(residual)
        specs.append(pl.BlockSpec((tm, tn), lambda i, j: (i, j)))
    return _pallas(
        functools.partial(_matmul_kernel, norm=norm, residual=residual is not None), ins,
        grid=(m // tm, n // tn),
        in_specs=specs,
        out_specs=pl.BlockSpec((tm, tn), lambda i, j: (i, j)),
        out_shape=jax.ShapeDtypeStruct((m, n), out_dtype),
        scratch_shapes=[pltpu.VMEM((tm, k), BF16)] if norm else [],
        sem=("parallel", "arbitrary"),
        name=name,
        casts=casts,
    )


def _ffn_in_kernel(x_ref, g_ref, wg_ref, wu_ref, cw_ref, cb_ref, o_ref, xn_ref, gate_ref, carry_ref,
                   *, tm, tiles_per_seq):
    i = pl.program_id(0)
    j = pl.program_id(1)

    @pl.when(j == 0)
    def _():
        xn_ref[...] = _rms(x_ref[...], g_ref[...]).astype(BF16)

    seq_start = i % tiles_per_seq == 0

    @pl.when(seq_start)
    def _():
        gate_ref[:CONV_HALO, :] = jnp.zeros((CONV_HALO, gate_ref.shape[1]), F32)

    @pl.when(jnp.logical_not(seq_start))
    def _():
        gate_ref[:CONV_HALO, :] = carry_ref[j]

    gate_ref[CONV_HALO:, :] = _dot(xn_ref[...], wg_ref[...])
    up = _dot(xn_ref[...], wu_ref[...])
    carry_ref[j] = gate_ref[tm:, :]
    cw = cw_ref[...]
    conv = cb_ref[...] + gate_ref[pl.ds(CONV_HALO - 2, tm), :] * cw[0:1, :]
    conv = conv + gate_ref[pl.ds(CONV_HALO - 1, tm), :] * cw[1:2, :]
    conv = conv + gate_ref[pl.ds(CONV_HALO, tm), :] * cw[2:3, :]
    o_ref[...] = (conv * _sigmoid(conv) * up).astype(o_ref.dtype)


def _ffn_in(h, gain, w_in, layer, conv_w, conv_b, *, seq, tm, tn, casts=()):
    m, k = h.shape
    d_ff = w_in.shape[-1] // 2
    tm = min(tm, seq)
    assert seq % tm == 0 and d_ff % tn == 0 and tm % CONV_HALO == 0
    nj = d_ff // tn
    return _pallas(
        functools.partial(_ffn_in_kernel, tm=tm, tiles_per_seq=seq // tm),
        (h, gain.reshape(1, k), w_in, w_in, conv_w, conv_b.reshape(1, d_ff)),
        grid=(m // tm, nj),
        in_specs=[
            pl.BlockSpec((tm, k), lambda i, j: (i, 0)),
            pl.BlockSpec((1, k), lambda i, j: (0, 0)),
            _wspec(k, tn, layer, lambda i, j: j),
            _wspec(k, tn, layer, lambda i, j: j + nj),
            pl.BlockSpec((CONV_W, tn), lambda i, j: (0, j)),
            pl.BlockSpec((1, tn), lambda i, j: (0, j)),
        ],
        out_specs=pl.BlockSpec((tm, tn), lambda i, j: (i, j)),
        out_shape=jax.ShapeDtypeStruct((m, d_ff), BF16),
        scratch_shapes=[pltpu.VMEM((tm, k), BF16), pltpu.VMEM((tm + CONV_HALO, tn), F32),
                        pltpu.VMEM((nj, CONV_HALO, tn), F32)],
        sem=("arbitrary", "arbitrary"),
        name="ffn_in",
        casts=casts,
    )


def _ple_kernel(*refs, with_next):
    if with_next:
        h_ref, g_ref, wg_ref, p_ref, wp_ref, gn_ref, o_ref, xn_ref = refs
    else:
        h_ref, g_ref, wg_ref, p_ref, wp_ref, o_ref = refs
    h = h_ref[...]
    proj = _dot(p_ref[...].astype(BF16), wp_ref[...])
    gate = _dot(_rms(h, g_ref[...]).astype(BF16), wg_ref[...])
    out = h + proj * _sigmoid(gate)
    o_ref[...] = out
    if with_next:
        xn_ref[...] = _rms(out, gn_ref[...]).astype(BF16)


def _ple(h, gain, w_gate, p, w_proj, layer, *, tm, next_gain=None):
    m, k = h.shape
    n = w_gate.shape[-1]
    kp = p.shape[2]
    tm = min(tm, m)
    assert m % tm == 0 and n == k
    gate_layer = layer if w_gate.ndim == 3 else None
    with_next = next_gain is not None
    row = pl.BlockSpec((tm, n), lambda i: (i, 0))
    ins = [h, gain.reshape(1, k), w_gate, p, w_proj]
    specs = [
        pl.BlockSpec((tm, k), lambda i: (i, 0)),
        pl.BlockSpec((1, k), lambda i: (0, 0)),
        _wspec(k, n, gate_layer, lambda i: 0, resident=True),
        pl.BlockSpec((None, tm, kp), lambda i: (layer, i, 0)),
        _wspec(kp, n, layer, lambda i: 0, resident=True),
    ]
    if with_next:
        ins.append(next_gain.reshape(1, n))
        specs.append(pl.BlockSpec((1, n), lambda i: (0, 0)))
    return pl.pallas_call(
        functools.partial(_ple_kernel, with_next=with_next),
        grid=(m // tm,),
        in_specs=specs,
        out_specs=[row, row] if with_next else row,
        out_shape=([jax.ShapeDtypeStruct((m, n), F32), jax.ShapeDtypeStruct((m, n), BF16)] if with_next
                   else jax.ShapeDtypeStruct((m, n), F32)),
        compiler_params=_params("parallel"),
        name="ple",
    )(*ins)


def _rope_table_kernel(pos_ref, inv_ref, c_ref, s_ref):
    ang = pos_ref[...].astype(F32) * inv_ref[...]
    lane = lax.broadcasted_iota(jnp.int32, ang.shape, 1)
    c = jnp.cos(ang)
    s = jnp.sin(ang)
    half = QK_ROPE // 2
    c_ref[...] = jnp.where(lane < QK_ROPE, c, 0.0)
    s_ref[...] = jnp.where(lane < half, -s, jnp.where(lane < QK_ROPE, s, 0.0))


def _rope_tables(positions, *, tm):
    m = positions.size
    tm = min(tm, m)
    half = QK_ROPE // 2
    inv_freq = ROPE_THETA ** (-jnp.arange(0, QK_ROPE, 2, dtype=F32) / QK_ROPE)
    inv_row = jnp.concatenate([inv_freq, inv_freq, jnp.zeros((LANES - 2 * half,), F32)]).reshape(1, LANES)
    out = jax.ShapeDtypeStruct((m, LANES), F32)
    spec = pl.BlockSpec((tm, LANES), lambda i: (i, 0))
    return pl.pallas_call(
        _rope_table_kernel,
        grid=(m // tm,),
        in_specs=[pl.BlockSpec((tm, 1), lambda i: (i, 0)), pl.BlockSpec((1, LANES), lambda i: (0, 0))],
        out_specs=[spec, spec],
        out_shape=[out, out],
        compiler_params=_params("parallel"),
        name="rope_tables",
    )(positions.reshape(m, 1), inv_row)


def _mla_proj_kernel(h_ref, h0_ref, g_ref, wa_ref, qn_ref, kvn_ref, wuq_ref, wukv_ref, gq_ref, gk_ref,
                     c_ref, s_ref, q_out, k_out, v_out, qraw_ref, kvraw_ref, kpe_ref):
    step = pl.program_id(0)

    def project(x, dst):
        xn = _rms(x, g_ref[...]).astype(BF16)
        c = _dot(xn, wa_ref[...])
        cq = _rms(c[:, :Q_LORA], qn_ref[...]).astype(BF16)
        ckv = _rms(c[:, Q_LORA:Q_LORA + KV_LORA], kvn_ref[...]).astype(BF16)
        kpe_ref[dst] = c[:, Q_LORA + KV_LORA:]
        qraw_ref[dst] = _dot(cq, wuq_ref[...])
        kvraw_ref[dst] = _dot(ckv, wukv_ref[...])

    def finish(slot):
        k_pe = kpe_ref[slot]
        cos_t, sin_t = c_ref[...], s_ref[...]

        def rope(r):
            return r * cos_t + pltpu.roll(r, LANES // 2, 1) * sin_t

        def row_total(sq):
            return jnp.sum(sq, axis=-1, keepdims=True)

        gq = gq_ref[...]
        gk = gk_ref[...]
        k_pe_sq = row_total(0.5 * (k_pe * k_pe))
        k_rope = rope(k_pe * gk[:, QK_NOPE:])

        for hd in range(MLA_HEADS):
            lo = hd * HEAD_PAD
            mid = lo + QK_NOPE
            hi = lo + HEAD_PAD
            q_nope, q_rope = qraw_ref[slot, :, lo:mid], qraw_ref[slot, :, mid:hi]
            q_sq = row_total(q_nope * q_nope + 0.5 * (q_rope * q_rope))
            inv = lax.rsqrt(q_sq * (1.0 / QK_HEAD) + RMS_EPS)
            q_out[:, lo:mid] = (q_nope * inv * gq[:, :QK_NOPE]).astype(BF16)
            q_out[:, mid:hi] = rope(q_rope * inv * gq[:, QK_NOPE:]).astype(BF16)

            kn = kvraw_ref[slot, :, lo:mid]
            inv_k = lax.rsqrt((row_total(kn * kn) + k_pe_sq) * (1.0 / QK_HEAD) + RMS_EPS)
            k_out[:, lo:mid] = (kn * inv_k * gk[:, :QK_NOPE]).astype(BF16)
            k_out[:, mid:hi] = (k_rope * inv_k).astype(BF16)
            v_out[:, hd * V_HEAD:(hd + 1) * V_HEAD] = kvraw_ref[slot, :, mid:hi].astype(BF16)

    @pl.when(step == 0)
    def _():
        project(h0_ref[...], 0)

    for cur in range(2):
        @pl.when(step % 2 == cur)
        def _(cur=cur):
            project(h_ref[...], 1 - cur)
            finish(cur)


def _mla_proj(h, gain, w_a, q_norm, w_uq_pad, kv_norm, w_ukv, layer, gq_pad, gk_pad, tables, *, tm, casts=()):
    m, k = h.shape
    tm = min(tm, m)
    assert m % tm == 0
    a_dim = w_a.shape[2]
    nq = MLA_HEADS * HEAD_PAD
    steps = m // tm
    const = lambda i: (0, 0)
    row = lambda i: (i, 0)
    return _pallas(
        _mla_proj_kernel,
        (h, h, gain.reshape(1, k), w_a, q_norm.reshape(1, Q_LORA), kv_norm.reshape(1, KV_LORA),
         w_uq_pad, w_ukv, gq_pad, gk_pad, *tables),
        grid=(steps,),
        in_specs=[
            pl.BlockSpec((tm, k), lambda i: (jnp.minimum(i + 1, steps - 1), 0)),
            pl.BlockSpec((tm, k), const, pipeline_mode=pl.Buffered(1)),
            pl.BlockSpec((1, k), const),
            _wspec(k, a_dim, layer, lambda i: 0, resident=True),
            pl.BlockSpec((1, Q_LORA), const),
            pl.BlockSpec((1, KV_LORA), const),
            _wspec(Q_LORA, nq, layer, lambda i: 0, resident=True),
            _wspec(KV_LORA, nq, layer, lambda i: 0, resident=True),
            pl.BlockSpec((1, HEAD_PAD), const),
            pl.BlockSpec((1, HEAD_PAD), const),
            pl.BlockSpec((tm, LANES), row),
            pl.BlockSpec((tm, LANES), row),
        ],
        out_specs=[pl.BlockSpec((tm, nq), row), pl.BlockSpec((tm, nq), row),
                   pl.BlockSpec((tm, MLA_HEADS * V_HEAD), row)],
        out_shape=[jax.ShapeDtypeStruct((m, nq), BF16), jax.ShapeDtypeStruct((m, nq), BF16),
                   jax.ShapeDtypeStruct((m, MLA_HEADS * V_HEAD), BF16)],
        scratch_shapes=[pltpu.VMEM((2, tm, nq), F32),
                        pltpu.VMEM((2, tm, nq), F32),
                        pltpu.VMEM((2, tm, LANES), F32)],
        sem=("arbitrary",),
        name="mla_proj",
        casts=casts,
    )


def _attn_kernel(q_ref, k_ref, v_ref, o_ref, vx_ref, *, blk):
    seq = q_ref.shape[0]
    row = lax.broadcasted_iota(jnp.int32, (blk, blk), 0)
    col = lax.broadcasted_iota(jnp.int32, (blk, blk), 1)
    causal = col <= row
    vx_ref[:, :V_HEAD] = v_ref[...]
    vx_ref[:, V_HEAD:] = jnp.ones((seq, V_HEAD), BF16)
    for qi in range(seq // blk):
        q = q_ref[qi * blk:(qi + 1) * blk, :]
        m_run = jnp.full((blk, 1), -jnp.inf, F32)
        acc = jnp.zeros((blk, 2 * V_HEAD), F32)
        for ki in range(qi + 1):
            s = _dot_nt(q, k_ref[ki * blk:(ki + 1) * blk, :])
            if ki == qi:
                s = jnp.where(causal, s, -jnp.inf)
            m_new = jnp.maximum(m_run, jnp.max(s, axis=-1, keepdims=True))
            p = jnp.exp2(s - m_new)
            acc = jnp.exp2(m_run - m_new) * acc + _dot(p.astype(BF16), vx_ref[ki * blk:(ki + 1) * blk, :])
            m_run = m_new
        o_ref[qi * blk:(qi + 1) * blk, :] = (acc[:, :V_HEAD] / acc[:, V_HEAD:]).astype(o_ref.dtype)


def _attention(q, k, v, *, batch, seq, blk, casts=()):
    m = q.shape[0]
    blk = min(blk, seq)
    assert seq % blk == 0
    return _pallas(
        functools.partial(_attn_kernel, blk=blk), (q, k, v),
        grid=(batch, MLA_HEADS),
        in_specs=[
            pl.BlockSpec((seq, HEAD_PAD), lambda b, h: (b, h)),
            pl.BlockSpec((seq, HEAD_PAD), lambda b, h: (b, h)),
            pl.BlockSpec((seq, V_HEAD), lambda b, h: (b, h)),
        ],
        out_specs=pl.BlockSpec((seq, V_HEAD), lambda b, h: (b, h)),
        out_shape=jax.ShapeDtypeStruct((m, MLA_HEADS * V_HEAD), BF16),
        scratch_shapes=[pltpu.VMEM((seq, 2 * V_HEAD), BF16)],
        sem=("parallel", "parallel"),
        name="mla_attention",
        casts=casts,
    )


def _hgrn_prep(z_ref, slot, lb, tri, qs_ref, cs_ref, bs_ref, qi_ref, kd_ref, eb_ref, *, chunks):
    width = lb.shape[1]
    for c in range(chunks):
        rows = slice(c * CHUNK, (c + 1) * CHUNK)
        ff = z_ref[slot, rows, width:2 * width]
        e_abs = jnp.exp(-jnp.abs(ff))
        r_abs = 1.0 / (1.0 + e_abs)
        f = lb + (1.0 - lb) * jnp.where(ff >= 0.0, r_abs, e_abs * r_abs)
        log_f = jnp.log2(f)
        log_k = jnp.log2(1.0 - lb) - (jnp.maximum(ff, 0.0) * LOG2_E + jnp.log2(1.0 + e_abs))
        q_raw = z_ref[slot, rows, 0:width]
        q = q_raw * _sigmoid(q_raw)
        hi = log_f.astype(BF16)
        r1 = log_f - hi.astype(F32)
        mid = r1.astype(BF16)
        lo = (r1 - mid.astype(F32)).astype(BF16)
        parts = _dot(tri, jnp.concatenate([hi, mid, lo], axis=1))
        b = parts[:, :width] + parts[:, width:2 * width] + parts[:, 2 * width:]
        b_last = b[CHUNK - 1:CHUNK, :]
        c_dec = b - log_k
        qs_ref[rows, :] = q
        cs_ref[rows, :] = c_dec
        bs_ref[rows, :] = b
        qi_ref[rows, :] = (q * jnp.exp2(b)).astype(BF16)
        kd_ref[rows, :] = jnp.exp2(b_last - c_dec).astype(BF16)
        eb_ref[c] = jnp.exp2(b_last)


def _hgrn_intra(q, c_dec, b):
    lane = lax.broadcasted_iota(jnp.int32, (HALF, CHUNK), 1)
    srow = lax.broadcasted_iota(jnp.int32, (CHUNK, HGRN_DK), 0)
    a_rows = []
    for blk in range(CHUNK // SUB):
        r0 = blk * SUB
        halves = []
        for hf in range(2):
            t0 = r0 + hf * HALF
            q_t = q[t0:t0 + HALF, :]
            b_t = b[t0:t0 + HALF, :]
            a_half = jnp.zeros((HALF, CHUNK), F32)
            for sl in range((hf + 1) * HALF):
                s = r0 + sl
                col = jnp.sum(q_t * jnp.exp2(b_t - c_dec[s:s + 1, :]), axis=-1, keepdims=True)
                a_half = jnp.where(lane == s, col, a_half)
            halves.append(a_half)
        a_blk = jnp.concatenate(halves, axis=0)
        if blk > 0:
            ref_b = b[r0 - 1:r0, :]
            q_f = (q[r0:r0 + SUB, :] * jnp.exp2(b[r0:r0 + SUB, :] - ref_b)).astype(BF16)
            k_f = jnp.where(srow < r0, jnp.exp2(ref_b - c_dec), 0.0).astype(BF16)
            a_blk = a_blk + _dot_nt(q_f, k_f)
        a_rows.append(a_blk)
    t_idx = lax.broadcasted_iota(jnp.int32, (CHUNK, CHUNK), 0)
    s_idx = lax.broadcasted_iota(jnp.int32, (CHUNK, CHUNK), 1)
    return jnp.where(s_idx <= t_idx, jnp.concatenate(a_rows, axis=0), 0.0)


def _hgrn_kernel(xn_ref, x0_ref, w0_ref, w1_ref, w2_ref, w3_ref, lb_ref, on_ref, o_ref,
                 z_ref, st_ref, qs_ref, cs_ref, bs_ref, qi_ref, kd_ref, eb_ref, a_ref, *, heads, chunks, nt):
    step = pl.program_id(0)
    w_refs = (w0_ref, w1_ref, w2_ref, w3_ref)
    width = heads * HGRN_DK

    @pl.when(step == 0)
    def _():
        x0 = x0_ref[...]
        for part, w_ref in enumerate(w_refs):
            z_ref[0, :, part * width:(part + 1) * width] = _dot(x0, w_ref[...])

    @pl.when(step % nt == 0)
    def _():
        st_ref[...] = jnp.zeros_like(st_ref)

    r = lax.broadcasted_iota(jnp.int32, (CHUNK, CHUNK), 0)
    c = lax.broadcasted_iota(jnp.int32, (CHUNK, CHUNK), 1)
    tri = (c <= r).astype(BF16)

    def build_a(ci):
        rows = slice(ci * CHUNK, (ci + 1) * CHUNK)
        for hd in range(heads):
            cols = slice(hd * HGRN_DK, (hd + 1) * HGRN_DK)
            a_ref[ci % 2, hd] = _hgrn_intra(qs_ref[rows, cols], cs_ref[rows, cols], bs_ref[rows, cols]).astype(BF16)

    def finish(ci, slot):
        rows = slice(ci * CHUNK, (ci + 1) * CHUNK)
        for hd in range(heads):
            cols = slice(hd * HGRN_DK, (hd + 1) * HGRN_DK)
            v16 = z_ref[slot, rows, 2 * width + hd * HGRN_DK:2 * width + (hd + 1) * HGRN_DK].astype(BF16)
            st = st_ref[hd]
            o = _dot_nt(qi_ref[rows, cols], st.astype(BF16)) + _dot(a_ref[ci % 2, hd], v16)
            st_ref[hd] = st * eb_ref[ci, :, cols] + _dot_tn(v16, kd_ref[rows, cols])
            o = o * lax.rsqrt(jnp.mean(o * o, axis=-1, keepdims=True) + RMS_EPS) * on_ref[:, cols]
            gate = z_ref[slot, rows, 3 * width + hd * HGRN_DK:3 * width + (hd + 1) * HGRN_DK]
            o_ref[rows, cols] = (o * (gate * _sigmoid(gate))).astype(o_ref.dtype)

    def run(slot):
        def project_next(part):
            z_ref[1 - slot, :, part * width:(part + 1) * width] = _dot(xn_ref[...], w_refs[part][...])

        project_next(0)
        _hgrn_prep(z_ref, slot, lb_ref[...], tri, qs_ref, cs_ref, bs_ref, qi_ref, kd_ref, eb_ref, chunks=chunks)
        build_a(0)
        rest = len(w_refs) - 1
        starts = {(i * chunks + rest // 2) // rest: i + 1 for i in range(rest)}
        for ci in range(chunks):
            if ci in starts:
                project_next(starts[ci])
            finish(ci, slot)
            if ci + 1 < chunks:
                build_a(ci + 1)

    for cur in range(2):
        @pl.when(step % 2 == cur)
        def _(cur=cur):
            run(cur)


def _hgrn_mixer_core(xn, w_in, lb, o_gain, *, batch, seq, heads, tb, casts=()):
    m, k = xn.shape
    fdim = HGRN_HEADS * HGRN_DK
    tb = min(tb, seq)
    width = heads * HGRN_DK
    groups = HGRN_HEADS // heads
    nt = seq // tb
    chunks = tb // CHUNK
    n_parts = 4
    assert seq % tb == 0 and tb % CHUNK == 0 and HGRN_HEADS % heads == 0 and chunks % n_parts == 0 and nt >= 2
    steps = batch * groups * nt

    def rows_of(s):
        return (s // (nt * groups)) * nt + s % nt

    def group_of(s):
        return (s // nt) % groups

    def nxt(s):
        return jnp.minimum(s + 1, steps - 1)

    def wspec(part):
        return pl.BlockSpec((k, width), lambda s: (0, part * groups + group_of(nxt(s))))

    return _pallas(
        functools.partial(_hgrn_kernel, heads=heads, chunks=chunks, nt=nt),
        (xn, xn, w_in, w_in, w_in, w_in, lb.reshape(1, fdim), o_gain.reshape(1, fdim)),
        grid=(steps,),
        in_specs=[pl.BlockSpec((tb, k), lambda s: (rows_of(nxt(s)), 0)),
                  pl.BlockSpec((tb, k), lambda s: (0, 0), pipeline_mode=pl.Buffered(1)),
                  wspec(0), wspec(1), wspec(2), wspec(3),
                  pl.BlockSpec((1, width), lambda s: (0, group_of(s))),
                  pl.BlockSpec((1, width), lambda s: (0, group_of(s)))],
        out_specs=pl.BlockSpec((tb, width), lambda s: (rows_of(s), group_of(s))),
        out_shape=jax.ShapeDtypeStruct((m, fdim), BF16),
        scratch_shapes=[
            pltpu.VMEM((2, tb, n_parts * width), F32),
            pltpu.VMEM((heads, HGRN_DV, HGRN_DK), F32),
            pltpu.VMEM((tb, width), F32),
            pltpu.VMEM((tb, width), F32),
            pltpu.VMEM((tb, width), F32),
            pltpu.VMEM((tb, width), BF16),
            pltpu.VMEM((tb, width), BF16),
            pltpu.VMEM((chunks, 1, width), F32),
            pltpu.VMEM((2, heads, CHUNK, CHUNK), BF16),
        ],
        sem=("arbitrary",),
        name="hgrn_mixer",
        casts=casts,
    )


def _with_swapped_rope(t):
    half = QK_ROPE // 2
    return jnp.concatenate([t, t[..., -half:], t[..., -QK_ROPE:-half]], axis=-1)


def _mla_layer(h, tables, layer, mix_gain, w_a_pad, q_norm, w_uq_pad, kv_norm, w_ukv, gain_q, gain_k, w_o,
               *, batch, seq, casts=()):
    gq_pad = _with_swapped_rope(gain_q * (LOG2_E / math.sqrt(QK_HEAD))).reshape(1, HEAD_PAD)
    gk_pad = _with_swapped_rope(gain_k).reshape(1, HEAD_PAD)
    (q, k, v), (w_o16,) = _mla_proj(h, mix_gain, w_a_pad, q_norm, w_uq_pad, kv_norm, w_ukv, layer, gq_pad, gk_pad,
                                    tables, tm=256, casts=[(w_o, layer)])
    o, copies = _attention(q, k, v, batch=batch, seq=seq, blk=256, casts=casts)
    h, _ = _matmul(o, w_o16, None, residual=h, out_dtype=F32, tm=1024, tn=2048, name="mla_out")
    return h, copies


def _hgrn_layer(h, xn, layer, lb, w_in, o_gain, w_o, *, batch, seq, casts=()):
    o, copies = _hgrn_mixer_core(xn, w_in, lb, o_gain, batch=batch, seq=seq, heads=4, tb=512,
                                 casts=[(w_o, layer), *casts])
    h, _ = _matmul(o, copies[0], None, residual=h, out_dtype=F32, tm=1024, tn=2048, name="hgrn_out")
    return h, copies[1:]


def kernel(x, p, positions, mix_norm, ffn_norm, ple_norm, mla_w_a, mla_q_norm, mla_w_uq, mla_kv_norm, mla_w_ukv, mla_qk_gain_q, mla_qk_gain_k, mla_w_o, hgrn_lb_logits, hgrn_w_in, hgrn_o_norm, hgrn_w_o, ffn_w_in, ffn_conv_w, ffn_conv_b, ffn_w_down, ple_w_proj, ple_w_gate):
    batch, seq, d_model = x.shape
    depth = p.shape[0]
    m = batch * seq
    tables = _rope_tables(positions, tm=1024)
    sm = jax.nn.softmax(hgrn_lb_logits.astype(F32), axis=0)
    lower_bounds = jnp.cumsum(sm, axis=0) - sm[0]

    n_mla = mla_w_a.shape[0]
    w_a_pad = _with_swapped_rope(mla_w_a).astype(BF16)
    w_uq_pad = _with_swapped_rope(mla_w_uq.reshape(n_mla, Q_LORA, MLA_HEADS, QK_HEAD))
    w_uq_pad = w_uq_pad.reshape(n_mla, Q_LORA, MLA_HEADS * HEAD_PAD).astype(BF16)
    w_ukv = mla_w_ukv.astype(BF16)
    w_ple_proj = ple_w_proj.astype(BF16)
    p_rows = p.reshape(depth, m, -1)

    h = x.reshape(m, d_model)
    xn = w_hgrn_in = None
    for i in range(depth):
        j = i // 2
        hgrn_next = i + 1 < depth and (i + 1) % 2 == 1
        if i % 2 == 0:
            mixer_casts = [(ffn_w_in, i)] + ([(hgrn_w_in, (i + 1) // 2)] if hgrn_next else [])
            h, copies = _mla_layer(h, tables, j, mix_norm[i], w_a_pad, mla_q_norm[j], w_uq_pad, mla_kv_norm[j],
                                   w_ukv, mla_qk_gain_q[j], mla_qk_gain_k[j], mla_w_o, batch=batch, seq=seq,
                                   casts=mixer_casts)
            w_ffn_in = copies[0]
            w_hgrn_in = copies[1] if hgrn_next else None
        else:
            h, (w_ffn_in,) = _hgrn_layer(h, xn, j, lower_bounds[i], w_hgrn_in, hgrn_o_norm[j], hgrn_w_o,
                                         batch=batch, seq=seq, casts=[(ffn_w_in, i)])
        act, (w_ffn_down,) = _ffn_in(h, ffn_norm[i], w_ffn_in, None, ffn_conv_w[i], ffn_conv_b[i],
                                     seq=seq, tm=1024, tn=512, casts=[(ffn_w_down, i)])
        h, (w_ple_gate,) = _matmul(act, w_ffn_down, None, residual=h, out_dtype=F32, tm=1024, tn=512,
                                   name="ffn_down", casts=[(ple_w_gate, i)])
        if hgrn_next:
            h, xn = _ple(h, ple_norm[i], w_ple_gate, p_rows, w_ple_proj, i, tm=512, next_gain=mix_norm[i + 1])
        else:
            h = _ple(h, ple_norm[i], w_ple_gate, p_rows, w_ple_proj, i, tm=512)
    return h.reshape(batch, seq, d_model)
```

```python
import functools
import math

import jax
import jax.numpy as jnp
from jax import lax
from jax.experimental import pallas as pl
from jax.experimental.pallas import tpu as pltpu

F32 = jnp.float32
BF16 = jnp.bfloat16

RMS_EPS = 1e-6
ROPE_THETA = 10000.0
LOG2_E = math.log2(math.e)

LANES = 128
BF16_SUBLANES = 16
VMEM_LIMIT_BYTES = 56 * 1024 * 1024

MLA_HEADS = 16
Q_LORA = 512
KV_LORA = 512
QK_NOPE = 128
QK_ROPE = 64
QK_HEAD = QK_NOPE + QK_ROPE
V_HEAD = 128
HEAD_PAD = 256

HGRN_HEADS = 16
HGRN_DK = 128
HGRN_DV = 128
CHUNK = 64
SUB = 16
HALF = SUB // 2

CONV_W = 3
F32_SUBLANES = 8
CONV_HALO = F32_SUBLANES


def _params(*sem):
    return pltpu.CompilerParams(dimension_semantics=sem, vmem_limit_bytes=VMEM_LIMIT_BYTES)


def _rms(x, g):
    return x * lax.rsqrt(jnp.mean(x * x, axis=-1, keepdims=True) + RMS_EPS) * g


def _sigmoid(x):
    return 1.0 / (1.0 + jnp.exp(-x))


def _dot(a, b):
    return jnp.dot(a, b, preferred_element_type=F32)


def _dot_nt(a, b):
    return lax.dot_general(a, b, (((1,), (1,)), ((), ())), preferred_element_type=F32)


def _dot_tn(a, b):
    return lax.dot_general(a, b, (((0,), (0,)), ((), ())), preferred_element_type=F32)


def _wspec(k, tn, layer, col, resident=False):
    mode = dict(pipeline_mode=pl.Buffered(1)) if resident else {}
    if layer is None:
        return pl.BlockSpec((k, tn), lambda *g: (0, col(*g)), **mode)
    return pl.BlockSpec((None, k, tn), lambda *g: (layer, 0, col(*g)), **mode)


def _pallas(body, ins, *, grid, in_specs, out_specs, out_shape, sem, name, scratch_shapes=(), casts=()):
    single = not isinstance(out_shape, (list, tuple))
    out_specs = [out_specs] if single else list(out_specs)
    out_shape = [out_shape] if single else list(out_shape)
    n_in, n_out, n_cast = len(ins), len(out_shape), len(casts)
    steps = math.prod(grid)

    def step_of(*g):
        idx = 0
        for size, pos in zip(grid, g):
            idx = idx * size + pos
        return idx

    c_in, c_in_specs, c_out_specs, c_out_shape, c_shapes = [], [], [], [], []
    for stack, layer in casts:
        layers, rows, cols = stack.shape
        size = rows * cols
        view_cols = next(c for c in (cols, cols // 2, cols // 4, cols // 8)
                         if c % LANES == 0 and (size // c) % (steps * BF16_SUBLANES) == 0)
        view_rows = size // view_cols
        rb = view_rows // steps
        c_in.append(stack.reshape(layers, view_rows, view_cols))
        c_in_specs.append(pl.BlockSpec((None, rb, view_cols), lambda *g, layer=layer: (layer, step_of(*g), 0)))
        c_out_specs.append(pl.BlockSpec((rb, view_cols), lambda *g: (step_of(*g), 0)))
        c_out_shape.append(jax.ShapeDtypeStruct((view_rows, view_cols), BF16))
        c_shapes.append((rows, cols))

    def kernel(*refs):
        outs_at = n_in + n_cast
        copies_at = outs_at + n_out
        for src, dst in zip(refs[n_in:outs_at], refs[copies_at:copies_at + n_cast]):
            dst[...] = src[...].astype(BF16)
        body(*refs[:n_in], *refs[outs_at:copies_at], *refs[copies_at + n_cast:])

    res = pl.pallas_call(
        kernel,
        grid=grid,
        in_specs=[*in_specs, *c_in_specs],
        out_specs=[*out_specs, *c_out_specs],
        out_shape=[*out_shape, *c_out_shape],
        scratch_shapes=list(scratch_shapes),
        compiler_params=_params(*sem),
        name=name,
    )(*ins, *c_in)
    outs = res[0] if single else list(res[:n_out])
    return outs, [copy.reshape(shape) for copy, shape in zip(res[n_out:], c_shapes)]


def _matmul_kernel(*refs, norm, residual):
    it = iter(refs)
    x_ref = next(it)
    g_ref = next(it) if norm else None
    w_ref = next(it)
    r_ref = next(it) if residual else None
    o_ref = next(it)
    xn_ref = next(it) if norm else None

    if norm:
        @pl.when(pl.program_id(1) == 0)
        def _():
            xn_ref[...] = _rms(x_ref[...], g_ref[...]).astype(BF16)
        acc = _dot(xn_ref[...], w_ref[...])
    else:
        acc = _dot(x_ref[...], w_ref[...])
    if residual:
        acc = r_ref[...] + acc
    o_ref[...] = acc.astype(o_ref.dtype)


def _matmul(x, w, layer, *, gain=None, residual=None, out_dtype, tm, tn, name, casts=()):
    m, k = x.shape
    n = w.shape[-1]
    tm, tn = min(tm, m), min(tn, n)
    assert m % tm == 0 and n % tn == 0
    norm = gain is not None
    ins = [x]
    specs = [pl.BlockSpec((tm, k), lambda i, j: (i, 0))]
    if norm:
        ins.append(gain.reshape(1, k))
        specs.append(pl.BlockSpec((1, k), lambda i, j: (0, 0)))
    ins.append(w)
    specs.append(_wspec(k, tn, layer, lambda i, j: j, resident=tn == n))
    if residual is not None:
        ins.append(residual)
        specs.append(pl.BlockSpec((tm, tn), lambda i, j: (i, j)))
    return _pallas(
        functools.partial(_matmul_kernel, norm=norm, residual=residual is not None), ins,
        grid=(m // tm, n // tn),
        in_specs=specs,
        out_specs=pl.BlockSpec((tm, tn), lambda i, j: (i, j)),
        out_shape=jax.ShapeDtypeStruct((m, n), out_dtype),
        scratch_shapes=[pltpu.VMEM((tm, k), BF16)] if norm else [],
        sem=("parallel", "arbitrary"),
        name=name,
        casts=casts,
    )


def _ffn_in_kernel(x_ref, g_ref, wg_ref, wu_ref, cw_ref, cb_ref, o_ref, xn_ref, gate_ref, carry_ref,
                   *, tm, tiles_per_seq):
    i = pl.program_id(0)
    j = pl.program_id(1)

    @pl.when(j == 0)
    def _():
        xn_ref[...] = _rms(x_ref[...], g_ref[...]).astype(BF16)

    seq_start = i % tiles_per_seq == 0

    @pl.when(seq_start)
    def _():
        gate_ref[:CONV_HALO, :] = jnp.zeros((CONV_HALO, gate_ref.shape[1]), F32)

    @pl.when(jnp.logical_not(seq_start))
    def _():
        gate_ref[:CONV_HALO, :] = carry_ref[j]

    gate_ref[CONV_HALO:, :] = _dot(xn_ref[...], wg_ref[...])
    up = _dot(xn_ref[...], wu_ref[...])
    carry_ref[j] = gate_ref[tm:, :]
    cw = cw_ref[...]
    conv = cb_ref[...] + gate_ref[pl.ds(CONV_HALO - 2, tm), :] * cw[0:1, :]
    conv = conv + gate_ref[pl.ds(CONV_HALO - 1, tm), :] * cw[1:2, :]
    conv = conv + gate_ref[pl.ds(CONV_HALO, tm), :] * cw[2:3, :]
    o_ref[...] = (conv * _sigmoid(conv) * up).astype(o_ref.dtype)


def _ffn_in(h, gain, w_in, layer, conv_w, conv_b, *, seq, tm, tn, casts=()):
    m, k = h.shape
    d_ff = w_in.shape[-1] // 2
    tm = min(tm, seq)
    assert seq % tm == 0 and d_ff % tn == 0 and tm % CONV_HALO == 0
    nj = d_ff // tn
    return _pallas(
        functools.partial(_ffn_in_kernel, tm=tm, tiles_per_seq=seq // tm),
        (h, gain.reshape(1, k), w_in, w_in, conv_w, conv_b.reshape(1, d_ff)),
        grid=(m // tm, nj),
        in_specs=[
            pl.BlockSpec((tm, k), lambda i, j: (i, 0)),
            pl.BlockSpec((1, k), lambda i, j: (0, 0)),
            _wspec(k, tn, layer, lambda i, j: j),
            _wspec(k, tn, layer, lambda i, j: j + nj),
            pl.BlockSpec((CONV_W, tn), lambda i, j: (0, j)),
            pl.BlockSpec((1, tn), lambda i, j: (0, j)),
        ],
        out_specs=pl.BlockSpec((tm, tn), lambda i, j: (i, j)),
        out_shape=jax.ShapeDtypeStruct((m, d_ff), BF16),
        scratch_shapes=[pltpu.VMEM((tm, k), BF16), pltpu.VMEM((tm + CONV_HALO, tn), F32),
                        pltpu.VMEM((nj, CONV_HALO, tn), F32)],
        sem=("arbitrary", "arbitrary"),
        name="ffn_in",
        casts=casts,
    )


def _ple_kernel(*refs, with_next):
    if with_next:
        h_ref, g_ref, wg_ref, p_ref, wp_ref, gn_ref, o_ref, xn_ref = refs
    else:
        h_ref, g_ref, wg_ref, p_ref, wp_ref, o_ref = refs
    h = h_ref[...]
    proj = _dot(p_ref[...].astype(BF16), wp_ref[...])
    gate = _dot(_rms(h, g_ref[...]).astype(BF16), wg_ref[...])
    out = h + proj * _sigmoid(gate)
    o_ref[...] = out
    if with_next:
        xn_ref[...] = _rms(out, gn_ref[...]).astype(BF16)


def _ple(h, gain, w_gate, p, w_proj, layer, *, tm, next_gain=None):
    m, k = h.shape
    n = w_gate.shape[-1]
    kp = p.shape[2]
    tm = min(tm, m)
    assert m % tm == 0 and n == k
    gate_layer = layer if w_gate.ndim == 3 else None
    with_next = next_gain is not None
    row = pl.BlockSpec((tm, n), lambda i: (i, 0))
    ins = [h, gain.reshape(1, k), w_gate, p, w_proj]
    specs = [
        pl.BlockSpec((tm, k), lambda i: (i, 0)),
        pl.BlockSpec((1, k), lambda i: (0, 0)),
        _wspec(k, n, gate_layer, lambda i: 0, resident=True),
        pl.BlockSpec((None, tm, kp), lambda i: (layer, i, 0)),
        _wspec(kp, n, layer, lambda i: 0, resident=True),
    ]
    if with_next:
        ins.append(next_gain.reshape(1, n))
        specs.append(pl.BlockSpec((1, n), lambda i: (0, 0)))
    return pl.pallas_call(
        functools.partial(_ple_kernel, with_next=with_next),
        grid=(m // tm,),
        in_specs=specs,
        out_specs=[row, row] if with_next else row,
        out_shape=([jax.ShapeDtypeStruct((m, n), F32), jax.ShapeDtypeStruct((m, n), BF16)] if with_next
                   else jax.ShapeDtypeStruct((m, n), F32)),
        compiler_params=_params("parallel"),
        name="ple",
    )(*ins)


def _rope_table_kernel(pos_ref, inv_ref, c_ref, s_ref):
    ang = pos_ref[...].astype(F32) * inv_ref[...]
    lane = lax.broadcasted_iota(jnp.int32, ang.shape, 1)
    c = jnp.cos(ang)
    s = jnp.sin(ang)
    half = QK_ROPE // 2
    c_ref[...] = jnp.where(lane < QK_ROPE, c, 0.0)
    s_ref[...] = jnp.where(lane < half, -s, jnp.where(lane < QK_ROPE, s, 0.0))


def _rope_tables(positions, *, tm):
    m = positions.size
    tm = min(tm, m)
    half = QK_ROPE // 2
    inv_freq = ROPE_THETA ** (-jnp.arange(0, QK_ROPE, 2, dtype=F32) / QK_ROPE)
    inv_row = jnp.concatenate([inv_freq, inv_freq, jnp.zeros((LANES - 2 * half,), F32)]).reshape(1, LANES)
    out = jax.ShapeDtypeStruct((m, LANES), F32)
    spec = pl.BlockSpec((tm, LANES), lambda i: (i, 0))
    return pl.pallas_call(
        _rope_table_kernel,
        grid=(m // tm,),
        in_specs=[pl.BlockSpec((tm, 1), lambda i: (i, 0)), pl.BlockSpec((1, LANES), lambda i: (0, 0))],
        out_specs=[spec, spec],
        out_shape=[out, out],
        compiler_params=_params("parallel"),
        name="rope_tables",
    )(positions.reshape(m, 1), inv_row)


def _mla_proj_kernel(h_ref, h0_ref, g_ref, wa_ref, qn_ref, kvn_ref, wuq_ref, wukv_ref, gq_ref, gk_ref,
                     c_ref, s_ref, q_out, k_out, v_out, qraw_ref, kvraw_ref, kpe_ref):
    step = pl.program_id(0)

    def project(x, dst):
        xn = _rms(x, g_ref[...]).astype(BF16)
        c = _dot(xn, wa_ref[...])
        cq = _rms(c[:, :Q_LORA], qn_ref[...]).astype(BF16)
        ckv = _rms(c[:, Q_LORA:Q_LORA + KV_LORA], kvn_ref[...]).astype(BF16)
        kpe_ref[dst] = c[:, Q_LORA + KV_LORA:]
        qraw_ref[dst] = _dot(cq, wuq_ref[...])
        kvraw_ref[dst] = _dot(ckv, wukv_ref[...])

    def finish(slot):
        k_pe = kpe_ref[slot]
        cos_t, sin_t = c_ref[...], s_ref[...]

        def rope(r):
            return r * cos_t + pltpu.roll(r, LANES // 2, 1) * sin_t

        def row_total(sq):
            return jnp.sum(sq, axis=-1, keepdims=True)

        gq = gq_ref[...]
        gk = gk_ref[...]
        k_pe_sq = row_total(0.5 * (k_pe * k_pe))
        k_rope = rope(k_pe * gk[:, QK_NOPE:])

        for hd in range(MLA_HEADS):
            lo = hd * HEAD_PAD
            mid = lo + QK_NOPE
            hi = lo + HEAD_PAD
            q_nope, q_rope = qraw_ref[slot, :, lo:mid], qraw_ref[slot, :, mid:hi]
            q_sq = row_total(q_nope * q_nope + 0.5 * (q_rope * q_rope))
            inv = lax.rsqrt(q_sq * (1.0 / QK_HEAD) + RMS_EPS)
            q_out[:, lo:mid] = (q_nope * inv * gq[:, :QK_NOPE]).astype(BF16)
            q_out[:, mid:hi] = rope(q_rope * inv * gq[:, QK_NOPE:]).astype(BF16)

            kn = kvraw_ref[slot, :, lo:mid]
            inv_k = lax.rsqrt((row_total(kn * kn) + k_pe_sq) * (1.0 / QK_HEAD) + RMS_EPS)
            k_out[:, lo:mid] = (kn * inv_k * gk[:, :QK_NOPE]).astype(BF16)
            k_out[:, mid:hi] = (k_rope * inv_k).astype(BF16)
            v_out[:, hd * V_HEAD:(hd + 1) * V_HEAD] = kvraw_ref[slot, :, mid:hi].astype(BF16)

    @pl.when(step == 0)
    def _():
        project(h0_ref[...], 0)

    for cur in range(2):
        @pl.when(step % 2 == cur)
        def _(cur=cur):
            project(h_ref[...], 1 - cur)
            finish(cur)


def _mla_proj(h, gain, w_a, q_norm, w_uq_pad, kv_norm, w_ukv, layer, gq_pad, gk_pad, tables, *, tm, casts=()):
    m, k = h.shape
    tm = min(tm, m)
    assert m % tm == 0
    a_dim = w_a.shape[2]
    nq = MLA_HEADS * HEAD_PAD
    steps = m // tm
    const = lambda i: (0, 0)
    row = lambda i: (i, 0)
    return _pallas(
        _mla_proj_kernel,
        (h, h, gain.reshape(1, k), w_a, q_norm.reshape(1, Q_LORA), kv_norm.reshape(1, KV_LORA),
         w_uq_pad, w_ukv, gq_pad, gk_pad, *tables),
        grid=(steps,),
        in_specs=[
            pl.BlockSpec((tm, k), lambda i: (jnp.minimum(i + 1, steps - 1), 0)),
            pl.BlockSpec((tm, k), const, pipeline_mode=pl.Buffered(1)),
            pl.BlockSpec((1, k), const),
            _wspec(k, a_dim, layer, lambda i: 0, resident=True),
            pl.BlockSpec((1, Q_LORA), const),
            pl.BlockSpec((1, KV_LORA), const),
            _wspec(Q_LORA, nq, layer, lambda i: 0, resident=True),
            _wspec(KV_LORA, nq, layer, lambda i: 0, resident=True),
            pl.BlockSpec((1, HEAD_PAD), const),
            pl.BlockSpec((1, HEAD_PAD), const),
            pl.BlockSpec((tm, LANES), row),
            pl.BlockSpec((tm, LANES), row),
        ],
        out_specs=[pl.BlockSpec((tm, nq), row), pl.BlockSpec((tm, nq), row),
                   pl.BlockSpec((tm, MLA_HEADS * V_HEAD), row)],
        out_shape=[jax.ShapeDtypeStruct((m, nq), BF16), jax.ShapeDtypeStruct((m, nq), BF16),
                   jax.ShapeDtypeStruct((m, MLA_HEADS * V_HEAD), BF16)],
        scratch_shapes=[pltpu.VMEM((2, tm, nq), F32),
                        pltpu.VMEM((2, tm, nq), F32),
                        pltpu.VMEM((2, tm, LANES), F32)],
        sem=("arbitrary",),
        name="mla_proj",
        casts=casts,
    )


def _attn_kernel(q_ref, k_ref, v_ref, o_ref, vx_ref, *, blk):
    seq = q_ref.shape[0]
    row = lax.broadcasted_iota(jnp.int32, (blk, blk), 0)
    col = lax.broadcasted_iota(jnp.int32, (blk, blk), 1)
    causal = col <= row
    vx_ref[:, :V_HEAD] = v_ref[...]
    vx_ref[:, V_HEAD:] = jnp.ones((seq, V_HEAD), BF16)
    for qi in range(seq // blk):
        q = q_ref[qi * blk:(qi + 1) * blk, :]
        m_run = jnp.full((blk, 1), -jnp.inf, F32)
        acc = jnp.zeros((blk, 2 * V_HEAD), F32)
        for ki in range(qi + 1):
            s = _dot_nt(q, k_ref[ki * blk:(ki + 1) * blk, :])
            if ki == qi:
                s = jnp.where(causal, s, -jnp.inf)
            m_new = jnp.maximum(m_run, jnp.max(s, axis=-1, keepdims=True))
            p = jnp.exp2(s - m_new)
            acc = jnp.exp2(m_run - m_new) * acc + _dot(p.astype(BF16), vx_ref[ki * blk:(ki + 1) * blk, :])
            m_run = m_new
        o_ref[qi * blk:(qi + 1) * blk, :] = (acc[:, :V_HEAD] / acc[:, V_HEAD:]).astype(o_ref.dtype)


def _attention(q, k, v, *, batch, seq, blk, casts=()):
    m = q.shape[0]
    blk = min(blk, seq)
    assert seq % blk == 0
    return _pallas(
        functools.partial(_attn_kernel, blk=blk), (q, k, v),
        grid=(batch, MLA_HEADS),
        in_specs=[
            pl.BlockSpec((seq, HEAD_PAD), lambda b, h: (b, h)),
            pl.BlockSpec((seq, HEAD_PAD), lambda b, h: (b, h)),
            pl.BlockSpec((seq, V_HEAD), lambda b, h: (b, h)),
        ],
        out_specs=pl.BlockSpec((seq, V_HEAD), lambda b, h: (b, h)),
        out_shape=jax.ShapeDtypeStruct((m, MLA_HEADS * V_HEAD), BF16),
        scratch_shapes=[pltpu.VMEM((seq, 2 * V_HEAD), BF16)],
        sem=("parallel", "parallel"),
        name="mla_attention",
        casts=casts,
    )


def _hgrn_prep(z_ref, slot, lb, tri, qs_ref, cs_ref, bs_ref, qi_ref, kd_ref, eb_ref, *, chunks):
    width = lb.shape[1]
    for c in range(chunks):
        rows = slice(c * CHUNK, (c + 1) * CHUNK)
        ff = z_ref[slot, rows, width:2 * width]
        e_abs = jnp.exp(-jnp.abs(ff))
        r_abs = 1.0 / (1.0 + e_abs)
        f = lb + (1.0 - lb) * jnp.where(ff >= 0.0, r_abs, e_abs * r_abs)
        log_f = jnp.log2(f)
        log_k = jnp.log2(1.0 - lb) - (jnp.maximum(ff, 0.0) * LOG2_E + jnp.log2(1.0 + e_abs))
        q_raw = z_ref[slot, rows, 0:width]
        q = q_raw * _sigmoid(q_raw)
        hi = log_f.astype(BF16)
        lo = (log_f - hi.astype(F32)).astype(BF16)
        parts = _dot(tri, jnp.concatenate([hi, lo], axis=1))
        b = parts[:, :width] + parts[:, width:]
        b_last = b[CHUNK - 1:CHUNK, :]
        c_dec = b - log_k
        qs_ref[rows, :] = q
        cs_ref[rows, :] = c_dec
        bs_ref[rows, :] = b
        qi_ref[rows, :] = (q * jnp.exp2(b)).astype(BF16)
        kd_ref[rows, :] = jnp.exp2(b_last - c_dec).astype(BF16)
        eb_ref[c] = jnp.exp2(b_last)


def _hgrn_intra(q, c_dec, b):
    lane = lax.broadcasted_iota(jnp.int32, (HALF, CHUNK), 1)
    srow = lax.broadcasted_iota(jnp.int32, (CHUNK, HGRN_DK), 0)
    a_rows = []
    for blk in range(CHUNK // SUB):
        r0 = blk * SUB
        halves = []
        for hf in range(2):
            t0 = r0 + hf * HALF
            q_t = q[t0:t0 + HALF, :]
            b_t = b[t0:t0 + HALF, :]
            a_half = jnp.zeros((HALF, CHUNK), F32)
            for sl in range((hf + 1) * HALF):
                s = r0 + sl
                col = jnp.sum(q_t * jnp.exp2(b_t - c_dec[s:s + 1, :]), axis=-1, keepdims=True)
                a_half = jnp.where(lane == s, col, a_half)
            halves.append(a_half)
        a_blk = jnp.concatenate(halves, axis=0)
        if blk > 0:
            ref_b = b[r0 - 1:r0, :]
            q_f = (q[r0:r0 + SUB, :] * jnp.exp2(b[r0:r0 + SUB, :] - ref_b)).astype(BF16)
            k_f = jnp.where(srow < r0, jnp.exp2(ref_b - c_dec), 0.0).astype(BF16)
            a_blk = a_blk + _dot_nt(q_f, k_f)
        a_rows.append(a_blk)
    t_idx = lax.broadcasted_iota(jnp.int32, (CHUNK, CHUNK), 0)
    s_idx = lax.broadcasted_iota(jnp.int32, (CHUNK, CHUNK), 1)
    return jnp.where(s_idx <= t_idx, jnp.concatenate(a_rows, axis=0), 0.0)


def _hgrn_kernel(xn_ref, x0_ref, w0_ref, w1_ref, w2_ref, w3_ref, lb_ref, on_ref, o_ref,
                 z_ref, st_ref, qs_ref, cs_ref, bs_ref, qi_ref, kd_ref, eb_ref, a_ref, *, heads, chunks, nt):
    step = pl.program_id(0)
    w_refs = (w0_ref, w1_ref, w2_ref, w3_ref)
    width = heads * HGRN_DK

    @pl.when(step == 0)
    def _():
        x0 = x0_ref[...]
        for part, w_ref in enumerate(w_refs):
            z_ref[0, :, part * width:(part + 1) * width] = _dot(x0, w_ref[...])

    @pl.when(step % nt == 0)
    def _():
        st_ref[...] = jnp.zeros_like(st_ref)

    r = lax.broadcasted_iota(jnp.int32, (CHUNK, CHUNK), 0)
    c = lax.broadcasted_iota(jnp.int32, (CHUNK, CHUNK), 1)
    tri = (c <= r).astype(BF16)

    def build_a(ci):
        rows = slice(ci * CHUNK, (ci + 1) * CHUNK)
        for hd in range(heads):
            cols = slice(hd * HGRN_DK, (hd + 1) * HGRN_DK)
            a_ref[ci % 2, hd] = _hgrn_intra(qs_ref[rows, cols], cs_ref[rows, cols], bs_ref[rows, cols]).astype(BF16)

    def finish(ci, slot):
        rows = slice(ci * CHUNK, (ci + 1) * CHUNK)
        for hd in range(heads):
            cols = slice(hd * HGRN_DK, (hd + 1) * HGRN_DK)
            v16 = z_ref[slot, rows, 2 * width + hd * HGRN_DK:2 * width + (hd + 1) * HGRN_DK].astype(BF16)
            st = st_ref[hd]
            o = _dot_nt(qi_ref[rows, cols], st.astype(BF16)) + _dot(a_ref[ci % 2, hd], v16)
            st_ref[hd] = st * eb_ref[ci, :, cols] + _dot_tn(v16, kd_ref[rows, cols])
            o = o * lax.rsqrt(jnp.mean(o * o, axis=-1, keepdims=True) + RMS_EPS) * on_ref[:, cols]
            gate = z_ref[slot, rows, 3 * width + hd * HGRN_DK:3 * width + (hd + 1) * HGRN_DK]
            o_ref[rows, cols] = (o * (gate * _sigmoid(gate))).astype(o_ref.dtype)

    def run(slot):
        def project_next(part):
            z_ref[1 - slot, :, part * width:(part + 1) * width] = _dot(xn_ref[...], w_refs[part][...])

        project_next(0)
        _hgrn_prep(z_ref, slot, lb_ref[...], tri, qs_ref, cs_ref, bs_ref, qi_ref, kd_ref, eb_ref, chunks=chunks)
        build_a(0)
        rest = len(w_refs) - 1
        starts = {(i * chunks + rest // 2) // rest: i + 1 for i in range(rest)}
        for ci in range(chunks):
            if ci in starts:
                project_next(starts[ci])
            finish(ci, slot)
            if ci + 1 < chunks:
                build_a(ci + 1)

    for cur in range(2):
        @pl.when(step % 2 == cur)
        def _(cur=cur):
            run(cur)


def _hgrn_mixer_core(xn, w_in, lb, o_gain, *, batch, seq, heads, tb, casts=()):
    m, k = xn.shape
    fdim = HGRN_HEADS * HGRN_DK
    tb = min(tb, seq)
    width = heads * HGRN_DK
    groups = HGRN_HEADS // heads
    nt = seq // tb
    chunks = tb // CHUNK
    n_parts = 4
    assert seq % tb == 0 and tb % CHUNK == 0 and HGRN_HEADS % heads == 0 and chunks % n_parts == 0 and nt >= 2
    steps = batch * groups * nt

    def rows_of(s):
        return (s // (nt * groups)) * nt + s % nt

    def group_of(s):
        return (s // nt) % groups

    def nxt(s):
        return jnp.minimum(s + 1, steps - 1)

    def wspec(part):
        return pl.BlockSpec((k, width), lambda s: (0, part * groups + group_of(nxt(s))))

    return _pallas(
        functools.partial(_hgrn_kernel, heads=heads, chunks=chunks, nt=nt),
        (xn, xn, w_in, w_in, w_in, w_in, lb.reshape(1, fdim), o_gain.reshape(1, fdim)),
        grid=(steps,),
        in_specs=[pl.BlockSpec((tb, k), lambda s: (rows_of(nxt(s)), 0)),
                  pl.BlockSpec((tb, k), lambda s: (0, 0), pipeline_mode=pl.Buffered(1)),
                  wspec(0), wspec(1), wspec(2), wspec(3),
                  pl.BlockSpec((1, width), lambda s: (0, group_of(s))),
                  pl.BlockSpec((1, width), lambda s: (0, group_of(s)))],
        out_specs=pl.BlockSpec((tb, width), lambda s: (rows_of(s), group_of(s))),
        out_shape=jax.ShapeDtypeStruct((m, fdim), BF16),
        scratch_shapes=[
            pltpu.VMEM((2, tb, n_parts * width), F32),
            pltpu.VMEM((heads, HGRN_DV, HGRN_DK), F32),
            pltpu.VMEM((tb, width), F32),
            pltpu.VMEM((tb, width), F32),
            pltpu.VMEM((tb, width), F32),
            pltpu.VMEM((tb, width), BF16),
            pltpu.VMEM((tb, width), BF16),
            pltpu.VMEM((chunks, 1, width), F32),
            pltpu.VMEM((2, heads, CHUNK, CHUNK), BF16),
        ],
        sem=("arbitrary",),
        name="hgrn_mixer",
        casts=casts,
    )


def _with_swapped_rope(t):
    half = QK_ROPE // 2
    return jnp.concatenate([t, t[..., -half:], t[..., -QK_ROPE:-half]], axis=-1)


def _mla_layer(h, tables, layer, mix_gain, w_a_pad, q_norm, w_uq_pad, kv_norm, w_ukv, gain_q, gain_k, w_o,
               *, batch, seq, casts=()):
    gq_pad = _with_swapped_rope(gain_q * (LOG2_E / math.sqrt(QK_HEAD))).reshape(1, HEAD_PAD)
    gk_pad = _with_swapped_rope(gain_k).reshape(1, HEAD_PAD)
    (q, k, v), (w_o16,) = _mla_proj(h, mix_gain, w_a_pad, q_norm, w_uq_pad, kv_norm, w_ukv, layer, gq_pad, gk_pad,
                                    tables, tm=256, casts=[(w_o, layer)])
    o, copies = _attention(q, k, v, batch=batch, seq=seq, blk=256, casts=casts)
    h, _ = _matmul(o, w_o16, None, residual=h, out_dtype=F32, tm=1024, tn=2048, name="mla_out")
    return h, copies


def _hgrn_layer(h, xn, layer, lb, w_in, o_gain, w_o, *, batch, seq, casts=()):
    o, copies = _hgrn_mixer_core(xn, w_in, lb, o_gain, batch=batch, seq=seq, heads=4, tb=512,
                                 casts=[(w_o, layer), *casts])
    h, _ = _matmul(o, copies[0], None, residual=h, out_dtype=F32, tm=1024, tn=2048, name="hgrn_out")
    return h, copies[1:]


def kernel(x, p, positions, mix_norm, ffn_norm, ple_norm, mla_w_a, mla_q_norm, mla_w_uq, mla_kv_norm, mla_w_ukv, mla_qk_gain_q, mla_qk_gain_k, mla_w_o, hgrn_lb_logits, hgrn_w_in, hgrn_o_norm, hgrn_w_o, ffn_w_in, ffn_conv_w, ffn_conv_b, ffn_w_down, ple_w_proj, ple_w_gate):
    batch, seq, d_model = x.shape
    depth = p.shape[0]
    m = batch * seq
    tables = _rope_tables(positions, tm=1024)
    sm = jax.nn.softmax(hgrn_lb_logits.astype(F32), axis=0)
    lower_bounds = jnp.cumsum(sm, axis=0) - sm[0]

    n_mla = mla_w_a.shape[0]
    w_a_pad = _with_swapped_rope(mla_w_a).astype(BF16)
    w_uq_pad = _with_swapped_rope(mla_w_uq.reshape(n_mla, Q_LORA, MLA_HEADS, QK_HEAD))
    w_uq_pad = w_uq_pad.reshape(n_mla, Q_LORA, MLA_HEADS * HEAD_PAD).astype(BF16)
    w_ukv = mla_w_ukv.astype(BF16)
    w_ple_proj = ple_w_proj.astype(BF16)
    p_rows = p.reshape(depth, m, -1)

    h = x.reshape(m, d_model)
    xn = w_hgrn_in = None
    for i in range(depth):
        j = i // 2
        hgrn_next = i + 1 < depth and (i + 1) % 2 == 1
        if i % 2 == 0:
            mixer_casts = [(ffn_w_in, i)] + ([(hgrn_w_in, (i + 1) // 2)] if hgrn_next else [])
            h, copies = _mla_layer(h, tables, j, mix_norm[i], w_a_pad, mla_q_norm[j], w_uq_pad, mla_kv_norm[j],
                                   w_ukv, mla_qk_gain_q[j], mla_qk_gain_k[j], mla_w_o, batch=batch, seq=seq,
                                   casts=mixer_casts)
            w_ffn_in = copies[0]
            w_hgrn_in = copies[1] if hgrn_next else None
        else:
            h, (w_ffn_in,) = _hgrn_layer(h, xn, j, lower_bounds[i], w_hgrn_in, hgrn_o_norm[j], hgrn_w_o,
                                         batch=batch, seq=seq, casts=[(ffn_w_in, i)])
        act, (w_ffn_down,) = _ffn_in(h, ffn_norm[i], w_ffn_in, None, ffn_conv_w[i], ffn_conv_b[i],
                                     seq=seq, tm=1024, tn=512, casts=[(ffn_w_down, i)])
        h, (w_ple_gate,) = _matmul(act, w_ffn_down, None, residual=h, out_dtype=F32, tm=1024, tn=512,
                                   name="ffn_down", casts=[(ple_w_gate, i)])
        if hgrn_next:
            h, xn = _ple(h, ple_norm[i], w_ple_gate, p_rows, w_ple_proj, i, tm=512, next_gain=mix_norm[i + 1])
        else:
            h = _ple(h, ple_norm[i], w_ple_gate, p_rows, w_ple_proj, i, tm=512)
    return h.reshape(batch, seq, d_model)
```
